```python
import math
import jax, jax.numpy as jnp
from jax import lax
import numpy as np


D_MODEL = 1024
BATCH = 8
SEQ = 4096
DEPTH = 1

CHUNK = 64
Q_BLOCK = 128
MIX_WIDTH = D_MODEL
DA_WIDTH = MIX_WIDTH // 2
DA_HEADS = 4
DA_V_DIM = DA_WIDTH // DA_HEADS
DA_QK_DIM = DA_V_DIM // 2
RET_WIDTH = MIX_WIDTH - DA_WIDTH
RET_HEADS = 4
RET_V_DIM = RET_WIDTH // RET_HEADS
RET_QK_DIM = RET_V_DIM // 2
SPLIT_SIZES = (DA_HEADS * 2 * DA_QK_DIM, DA_HEADS * 2 * DA_QK_DIM, DA_HEADS * DA_V_DIM,
               RET_HEADS * RET_QK_DIM, RET_HEADS * RET_QK_DIM, RET_HEADS * RET_V_DIM,
               RET_HEADS * RET_V_DIM)
IN_WIDTH = sum(SPLIT_SIZES)
D_FF = ((8 * D_MODEL // 3 + 127) // 128) * 128
CONV_WIDTH = 3
N_MOD = 6
RMS_EPS = 1e-6

kernel_name = 'hybrid_diffattn_retention_convffn_block'


def rms_norm(x, gain=None):
    xf = x.astype(jnp.float32)
    y = xf * lax.rsqrt(jnp.mean(xf * xf, axis=-1, keepdims=True) + RMS_EPS)
    if gain is not None:
        y = y * gain.astype(jnp.float32)
    return y.astype(x.dtype)


def split_cols(p, sizes):
    idx = []
    acc = 0
    for s in sizes[:-1]:
        acc += s
        idx.append(acc)
    return jnp.split(p, idx, axis=-1)


def diff_attention(q, k, v, lam, subln_gain, lambda_init):
    B, S, H, _, dq = q.shape
    dv = v.shape[-1]
    n_qb = S // Q_BLOCK
    scale = dq ** -0.5
    slopes = 2.0 ** (-8.0 * (jnp.arange(H, dtype=jnp.float32) + 1.0) / H)
    k_pos = jnp.arange(S)
    k_chunk = k_pos // CHUNK
    q_blocks = q.reshape(B, n_qb, Q_BLOCK, H, 2, dq).transpose(1, 0, 2, 3, 4, 5)

    def block(args):
        qb, start = args
        q_pos = start + jnp.arange(Q_BLOCK)
        s = jnp.einsum('bqhmd,bkhmd->bhmqk', qb, k).astype(jnp.float32) * scale
        dist = jnp.abs(q_pos[:, None] - k_pos[None, :]).astype(jnp.float32)
        bias = -slopes[:, None, None] * dist[None]
        allowed = k_chunk[None, :] <= (q_pos // CHUNK)[:, None]
        s = jnp.where(allowed, s + bias[None, :, None], -jnp.inf)
        p = jax.nn.softmax(s, axis=-1)
        a = p[:, :, 0] - lam * p[:, :, 1]
        return jnp.einsum('bhqk,bkhe->bqhe', a.astype(v.dtype), v)

    starts = jnp.arange(n_qb, dtype=jnp.int32) * Q_BLOCK
    o = lax.map(block, (q_blocks, starts))
    o = o.transpose(1, 0, 2, 3, 4).reshape(B, S, H, dv)
    o = rms_norm(o, subln_gain) * (1.0 - lambda_init)
    return o.reshape(B, S, H * dv)


def retention(q, k, v, g):
    B, S, H, dk = q.shape
    dv = v.shape[-1]
    N = S // CHUNK
    dt = q.dtype
    log_gamma = jnp.log(1.0 - 2.0 ** (-5.0 - jnp.arange(H, dtype=jnp.float32)))
    idx = jnp.arange(CHUNK, dtype=jnp.float32)
    intra_decay = jnp.exp(log_gamma[:, None, None] * jnp.abs(idx[:, None] - idx[None, :]))
    in_decay = jnp.exp(log_gamma[:, None] * (CHUNK - 1.0 - idx)).T
    out_decay = jnp.exp(log_gamma[:, None] * (idx + 1.0)).T
    chunk_decay = jnp.exp(log_gamma * CHUNK)

    qc = q.reshape(B, N, CHUNK, H, dk)
    kc = k.reshape(B, N, CHUNK, H, dk) * (dk ** -0.5)
    vc = v.reshape(B, N, CHUNK, H, dv)

    s = jnp.einsum('bnihd,bnjhd->bnhij', qc, kc) * intra_decay.astype(dt)
    o_intra = jnp.einsum('bnhij,bnjhe->bnihe', s, vc)

    u = jnp.einsum('bnjhd,bnjhe->nbhde', kc * in_decay[:, :, None].astype(dt), vc).astype(jnp.float32)

    def step(state, u_n):
        return chunk_decay[None, :, None, None] * state + u_n, state

    _, s_prev = lax.scan(step, jnp.zeros((B, H, dk, dv), jnp.float32), u)
    o_cross = jnp.einsum('bnihd,nbhde->bnihe', qc * out_decay[:, :, None].astype(dt), s_prev.astype(dt))

    o = (o_intra + o_cross).reshape(B, S, H, dv)
    o = rms_norm(o).reshape(B, S, H * dv)
    return o * jax.nn.silu(g)


def causal_dwconv(h, w, b):
    K = w.shape[0]
    S = h.shape[1]
    hp = jnp.pad(h, ((0, 0), (K - 1, 0), (0, 0)))
    out = b
    for j in range(K):
        out = out + hp[:, j:j + S] * w[j]
    return out


def setup_inputs(seed: int = 0) -> dict:
    key = jax.random.key(seed)
    ks = jax.random.split(key, 20)

    def nrm(k, shape, scale):
        return jax.random.normal(k, shape, jnp.float32) * scale

    def gain(k, n):
        return 1.0 + nrm(k, (DEPTH, n), 0.02)

    return {
        'x': nrm(ks[0], (BATCH, SEQ, D_MODEL), 1.0),
        'c': nrm(ks[1], (BATCH, D_MODEL), 1.0),
        'w_ada': nrm(ks[2], (DEPTH, D_MODEL, N_MOD * D_MODEL), 0.5 * D_MODEL ** -0.5),
        'b_ada': nrm(ks[3], (DEPTH, N_MOD * D_MODEL), 0.01),
        'g_pre_mix': gain(ks[4], D_MODEL),
        'w_in': nrm(ks[5], (DEPTH, D_MODEL, IN_WIDTH), D_MODEL ** -0.5),
        'lam_q1': nrm(ks[6], (DEPTH, DA_QK_DIM), 0.1),
        'lam_k1': nrm(ks[7], (DEPTH, DA_QK_DIM), 0.1),
        'lam_q2': nrm(ks[8], (DEPTH, DA_QK_DIM), 0.1),
        'lam_k2': nrm(ks[9], (DEPTH, DA_QK_DIM), 0.1),
        'g_da_subln': gain(ks[10], DA_V_DIM),
        'w_out': nrm(ks[11], (DEPTH, MIX_WIDTH, D_MODEL), MIX_WIDTH ** -0.5),
        'g_post_mix': gain(ks[12], D_MODEL),
        'g_pre_ffn': gain(ks[13], D_MODEL),
        'w_up': nrm(ks[14], (DEPTH, D_MODEL, 2 * D_FF), D_MODEL ** -0.5),
        'conv_w': nrm(ks[15], (DEPTH, CONV_WIDTH, 2 * D_FF), CONV_WIDTH ** -0.5),
        'conv_b': nrm(ks[16], (DEPTH, 2 * D_FF), 0.01),
        'w_down': nrm(ks[17], (DEPTH, D_FF, D_MODEL), D_FF ** -0.5),
        'g_post_ffn': gain(ks[18], D_MODEL),
    }


def reference(x, c, w_ada, b_ada, g_pre_mix, w_in, lam_q1, lam_k1, lam_q2, lam_k2, g_da_subln,
              w_out, g_post_mix, g_pre_ffn, w_up, conv_w, conv_b, w_down, g_post_ffn):
    B, S, _ = x.shape
    for l in range(DEPTH):
        mod = jax.nn.silu(c) @ w_ada[l] + b_ada[l]
        sh1, sc1, gt1, sh2, sc2, gt2 = [m[:, None, :] for m in jnp.split(mod, N_MOD, axis=-1)]

        h = rms_norm(x, g_pre_mix[l]) * (1.0 + sc1) + sh1
        proj = h @ w_in[l]
        da_q, da_k, da_v, r_q, r_k, r_v, r_g = split_cols(proj, SPLIT_SIZES)

        lambda_init = 0.8 - 0.6 * math.exp(-0.3 * l)
        lam = (jnp.exp(jnp.sum(lam_q1[l].astype(jnp.float32) * lam_k1[l].astype(jnp.float32)))
               - jnp.exp(jnp.sum(lam_q2[l].astype(jnp.float32) * lam_k2[l].astype(jnp.float32)))
               + lambda_init)
        o_da = diff_attention(da_q.reshape(B, S, DA_HEADS, 2, DA_QK_DIM),
                              da_k.reshape(B, S, DA_HEADS, 2, DA_QK_DIM),
                              da_v.reshape(B, S, DA_HEADS, DA_V_DIM),
                              lam, g_da_subln[l], lambda_init)
        o_ret = retention(r_q.reshape(B, S, RET_HEADS, RET_QK_DIM),
                          r_k.reshape(B, S, RET_HEADS, RET_QK_DIM),
                          r_v.reshape(B, S, RET_HEADS, RET_V_DIM),
                          r_g)
        mix = jnp.concatenate([o_da, o_ret], axis=-1) @ w_out[l]
        x = x + gt1 * rms_norm(mix, g_post_mix[l])

        h = rms_norm(x, g_pre_ffn[l]) * (1.0 + sc2) + sh2
        u = causal_dwconv(h @ w_up[l], conv_w[l], conv_b[l])
        u_gate, u_val = jnp.split(u, 2, axis=-1)
        f = (jax.nn.silu(u_gate) * u_val) @ w_down[l]
        x = x + gt2 * rms_norm(f, g_post_ffn[l])
    return x
```

```python
import functools
import math

import jax
import jax.numpy as jnp
from jax import lax
from jax.experimental import pallas as pl
from jax.experimental.pallas import tpu as pltpu

D_MODEL = 1024
CHUNK = 64
DA_HEADS = 4
DA_QK = 64
DA_V = 128
DA_WIDTH = 512
RET_HEADS = 4
RET_QK = 64
RET_V = 128
RET_WIDTH = 512
IN_WIDTH = 3072
D_FF = 2816
N_MOD = 6
RMS_EPS = 1e-6
LAMBDA_INIT = 0.8 - 0.6 * math.exp(-0.3 * 0)

LOG2E = 1.4426950408889634
Q_FOLD = (DA_QK ** -0.5) * LOG2E
RET_K_FOLD = RET_QK ** -0.5

VMEM_LIMIT_BYTES = 56 * 1024 * 1024

BF16 = jnp.bfloat16
F32 = jnp.float32

TM_IN = 512
TQ = 256
TK = 256
RET_L = 256
TM_FFN = 256
FF_CW = 256


def _rms(x):
    return x * lax.rsqrt(jnp.mean(x * x, axis=-1, keepdims=True) + RMS_EPS)


def _mod_kernel(c_ref, w_ref, b_ref, o_ref):
    c = c_ref[...]
    a = c * jax.nn.sigmoid(c)
    o_ref[...] = jnp.dot(a, w_ref[...], preferred_element_type=F32,
                         precision=lax.Precision.HIGHEST) + b_ref[...]


def _mod_call(c, w_ada, b_ada):
    b, d = c.shape
    n = w_ada.shape[1]
    tn = 1024
    return pl.pallas_call(
        _mod_kernel,
        grid=(n // tn,),
        in_specs=[pl.BlockSpec((b, d), lambda j: (0, 0)),
                  pl.BlockSpec((d, tn), lambda j: (0, j)),
                  pl.BlockSpec((1, tn), lambda j: (0, j))],
        out_specs=pl.BlockSpec((b, tn), lambda j: (0, j)),
        out_shape=jax.ShapeDtypeStruct((b, n), F32),
        compiler_params=pltpu.CompilerParams(dimension_semantics=("arbitrary",),
                                             vmem_limit_bytes=VMEM_LIMIT_BYTES),
        name="mod",
    )(c, w_ada, b_ada.reshape(1, n))


def _inproj_kernel(x_ref, mod_ref, g_ref, w_ref, wvt_ref, qk_ref, vt_ref, ret_ref):
    x = x_ref[0]
    sh = mod_ref[0, 0:1, :]
    sc = mod_ref[0, 1:2, :]
    h = (_rms(x) * g_ref[...]) * (1.0 + sc) + sh
    hb = h.astype(BF16)

    def proj(lo, hi):
        return jnp.dot(hb, w_ref[:, lo:hi], preferred_element_type=F32)

    qk_ref[0, :, 0:512] = (proj(0, 512) * Q_FOLD).astype(BF16)
    qk_ref[0, :, 512:1024] = proj(512, 1024).astype(BF16)
    vt = lax.dot_general(wvt_ref[...], hb, (((1,), (1,)), ((), ())),
                         preferred_element_type=F32).astype(BF16)
    for j in range(vt_ref.shape[1]):
        vt_ref[0, j] = vt[:, j * TK:(j + 1) * TK]
    ret_ref[0, :, 0:256] = proj(1536, 1792).astype(BF16)
    ret_ref[0, :, 256:512] = (proj(1792, 2048) * RET_K_FOLD).astype(BF16)
    ret_ref[0, :, 512:1024] = proj(2048, 2560).astype(BF16)
    ret_ref[0, :, 1024:1536] = proj(2560, 3072).astype(BF16)


def _inproj_call(x, mod, g_pre_mix, w_in_b, w_vt_b):
    b, s, d = x.shape
    tm = TM_IN
    const = dict(pipeline_mode=pl.Buffered(1))
    return pl.pallas_call(
        _inproj_kernel,
        grid=(b, s // tm),
        in_specs=[pl.BlockSpec((1, tm, d), lambda bi, i: (bi, i, 0)),
                  pl.BlockSpec((1, N_MOD, d), lambda bi, i: (bi, 0, 0)),
                  pl.BlockSpec((1, d), lambda bi, i: (0, 0)),
                  pl.BlockSpec((d, IN_WIDTH), lambda bi, i: (0, 0), **const),
                  pl.BlockSpec((DA_WIDTH, d), lambda bi, i: (0, 0), **const)],
        out_specs=[pl.BlockSpec((1, tm, 1024), lambda bi, i: (bi, i, 0)),
                   pl.BlockSpec((1, tm // TK, DA_WIDTH, TK), lambda bi, i: (bi, i, 0, 0)),
                   pl.BlockSpec((1, tm, 1536), lambda bi, i: (bi, i, 0))],
        out_shape=[jax.ShapeDtypeStruct((b, s, 1024), BF16),
                   jax.ShapeDtypeStruct((b, s // TK, DA_WIDTH, TK), BF16),
                   jax.ShapeDtypeStruct((b, s, 1536), BF16)],
        compiler_params=pltpu.CompilerParams(dimension_semantics=("arbitrary", "arbitrary"),
                                             vmem_limit_bytes=VMEM_LIMIT_BYTES),
        name="inproj",
    )(x, mod, g_pre_mix, w_in_b, w_vt_b)


def _dattn_kernel(q_ref, k_ref, vt_ref, lq1_ref, lk1_ref, lq2_ref, lk2_ref, gsub_ref,
                  o_ref, dbias_ref, cbias_ref, acc_ref):
    hd = pl.program_id(1)
    qi = pl.program_id(2)
    slope = jnp.where(hd == 0, 2.0 ** -2, jnp.where(hd == 1, 2.0 ** -4,
                      jnp.where(hd == 2, 2.0 ** -6, 2.0 ** -8))).astype(F32) * LOG2E

    @pl.when(qi == 0)
    def _():
        r = lax.broadcasted_iota(jnp.int32, (TK, 128), 0).astype(F32)
        cbias_ref[...] = slope * r
        c = lax.broadcasted_iota(jnp.int32, (TK, TQ), 0)
        a = lax.broadcasted_iota(jnp.int32, (TK, TQ), 1)
        val = slope * (a - jnp.abs(a - c)).astype(F32)
        allowed = (c // CHUNK) <= (a // CHUNK)
        dbias_ref[...] = jnp.where(allowed, val, -jnp.inf)

    q = q_ref[0]
    lane = lax.broadcasted_iota(jnp.int32, (TQ, 128), 1)
    zero = jnp.zeros_like(q)
    qq = jnp.concatenate([jnp.where(lane < DA_QK, q, zero),
                          jnp.where(lane >= DA_QK, q, zero)], axis=0)

    def scores(kb):
        kblk = k_ref[0, pl.ds(pl.multiple_of(kb * TK, TK), TK), :]
        return lax.dot_general(kblk, qq, (((1,), (1,)), ((), ())),
                               preferred_element_type=F32)

    def update(kb, t, m, l):
        mb = jnp.max(t, axis=0, keepdims=True)
        m_new = jnp.maximum(m, mb)
        alpha = jnp.exp2(m - m_new)
        p = jnp.exp2(t - m_new)
        l_new = alpha * l + jnp.sum(p, axis=0, keepdims=True)
        pv = jnp.dot(vt_ref[0, kb], p.astype(BF16), preferred_element_type=F32)
        acc_ref[...] = alpha * acc_ref[...] + pv
        return m_new, l_new

    acc_ref[...] = jnp.zeros_like(acc_ref)
    m0 = jnp.full((1, 2 * TQ), -jnp.inf, F32)
    l0 = jnp.zeros((1, 2 * TQ), F32)
    db = dbias_ref[...]
    t = scores(qi) + jnp.concatenate([db, db], axis=1)
    m, l = update(qi, t, m0, l0)

    def past(kb, carry):
        m, l = carry
        shift = slope * ((kb - qi) * TK).astype(F32)
        cb = cbias_ref[...] + shift
        t = scores(kb) + jnp.concatenate([cb] * (2 * TQ // 128), axis=1)
        return update(kb, t, m, l)

    m, l = lax.fori_loop(0, qi, past, (m, l))

    lam = (jnp.exp(jnp.sum(lq1_ref[...] * lk1_ref[...], axis=-1, keepdims=True))
           - jnp.exp(jnp.sum(lq2_ref[...] * lk2_ref[...], axis=-1, keepdims=True))
           + LAMBDA_INIT)
    inv = 1.0 / l
    acc = acc_ref[...]
    ot = acc[:, :TQ] * inv[:, :TQ] - lam * (acc[:, TQ:] * inv[:, TQ:])
    ot = ot * lax.rsqrt(jnp.mean(ot * ot, axis=0, keepdims=True) + RMS_EPS)
    o = ot.T * gsub_ref[...] * (1.0 - LAMBDA_INIT)
    o_ref[0] = o.astype(BF16)


def _dattn_call(qk, vt, lam_q1, lam_k1, lam_q2, lam_k2, g_sub):
    b, s, _ = qk.shape
    nkv = s // TK
    vec = lambda n: pl.BlockSpec((1, n), lambda bi, h, i: (0, 0))
    return pl.pallas_call(
        _dattn_kernel,
        grid=(b, DA_HEADS, s // TQ),
        in_specs=[pl.BlockSpec((1, TQ, 128), lambda bi, h, i: (bi, i, h)),
                  pl.BlockSpec((1, s, 128), lambda bi, h, i: (bi, 0, DA_HEADS + h)),
                  pl.BlockSpec((1, nkv, DA_V, TK), lambda bi, h, i: (bi, 0, h, 0)),
                  vec(DA_QK), vec(DA_QK), vec(DA_QK), vec(DA_QK), vec(DA_V)],
        out_specs=pl.BlockSpec((1, TQ, DA_V), lambda bi, h, i: (bi, i, h)),
        out_shape=jax.ShapeDtypeStruct((b, s, DA_WIDTH), BF16),
        scratch_shapes=[pltpu.VMEM((TK, TQ), F32),
                        pltpu.VMEM((TK, 128), F32),
                        pltpu.VMEM((DA_V, 2 * TQ), F32)],
        compiler_params=pltpu.CompilerParams(
            dimension_semantics=("arbitrary", "arbitrary", "arbitrary"),
            vmem_limit_bytes=VMEM_LIMIT_BYTES),
        name="dattn",
    )(qk, qk, vt, lam_q1, lam_k1, lam_q2, lam_k2, g_sub)


def _ret_kernel(q_ref, k_ref, v_ref, g_ref, o_ref, dec_ref, state_ref):
    i = pl.program_id(1)
    L = RET_L
    row = lax.broadcasted_iota(jnp.int32, (L, L), 0)
    col = lax.broadcasted_iota(jnp.int32, (L, L), 1)

    @pl.when(i == 0)
    def _():
        state_ref[...] = jnp.zeros_like(state_ref)
        d = row - col
        same = (row // CHUNK) == (col // CHUNK)
        past = (col // CHUNK) < (row // CHUNK)
        e = jnp.where(same, jnp.abs(d), d).astype(F32)
        for h in range(RET_HEADS):
            lg = math.log(1.0 - 2.0 ** (-5.0 - h))
            dec_ref[h] = jnp.where(same | past, jnp.exp(lg * e), 0.0)

    lane = lax.broadcasted_iota(jnp.int32, (L, 128), 1)
    r1 = lax.broadcasted_iota(jnp.int32, (L, 1), 0).astype(F32)
    for h in range(RET_HEADS):
        lg = math.log(1.0 - 2.0 ** (-5.0 - h))
        pair = slice(128 * (h // 2), 128 * (h // 2) + 128)
        lo = RET_QK * (h % 2)
        sel = (lane >= lo) & (lane < lo + RET_QK)
        qp = q_ref[0, :, pair]
        kp = k_ref[0, :, pair]
        qz = jnp.where(sel, qp, jnp.zeros_like(qp))
        kz = jnp.where(sel, kp, jnp.zeros_like(kp))
        vh = v_ref[0, :, 128 * h:128 * h + 128]
        s = lax.dot_general(qz, kz, (((1,), (1,)), ((), ())), preferred_element_type=F32)
        s = s * dec_ref[h]
        o = jnp.dot(s.astype(BF16), vh, preferred_element_type=F32)
        st = state_ref[h]
        out_dec = jnp.exp(lg * (r1 + 1.0))
        o = o + out_dec * jnp.dot(qz, st.astype(BF16), preferred_element_type=F32)
        in_dec = jnp.exp(lg * (L - 1.0 - r1))
        kd = (kz.astype(F32) * in_dec).astype(BF16)
        u = lax.dot_general(kd, vh, (((0,), (0,)), ((), ())), preferred_element_type=F32)
        state_ref[h] = math.exp(lg * L) * st + u
        g = g_ref[0, :, 128 * h:128 * h + 128].astype(F32)
        o = _rms(o) * (g * jax.nn.sigmoid(g))
        o_ref[0, :, 128 * h:128 * h + 128] = o.astype(BF16)


def _ret_call(ret):
    b, s, _ = ret.shape
    L = RET_L
    return pl.pallas_call(
        _ret_kernel,
        grid=(b, s // L),
        in_specs=[pl.BlockSpec((1, L, 256), lambda bi, i: (bi, i, 0)),
                  pl.BlockSpec((1, L, 256), lambda bi, i: (bi, i, 1)),
                  pl.BlockSpec((1, L, 512), lambda bi, i: (bi, i, 1)),
                  pl.BlockSpec((1, L, 512), lambda bi, i: (bi, i, 2))],
        out_specs=pl.BlockSpec((1, L, RET_WIDTH), lambda bi, i: (bi, i, 0)),
        out_shape=jax.ShapeDtypeStruct((b, s, RET_WIDTH), BF16),
        scratch_shapes=[pltpu.VMEM((RET_HEADS, L, L), F32),
                        pltpu.VMEM((RET_HEADS, 128, RET_V), F32)],
        compiler_params=pltpu.CompilerParams(dimension_semantics=("arbitrary", "arbitrary"),
                                             vmem_limit_bytes=VMEM_LIMIT_BYTES),
        name="ret",
    )(ret, ret, ret, ret)


def _ffn_kernel(oda_ref, oret_ref, x_ref, mod_ref, gpm_ref, gpf_ref, gpo_ref,
                wout_ref, wup_ref, cw_ref, cb_ref, wdn_ref, o_ref, ybuf_ref, carry_ref):
    i = pl.program_id(1)
    tm = x_ref.shape[1]

    @pl.when(i == 0)
    def _():
        carry_ref[...] = jnp.zeros_like(carry_ref)

    gt1 = mod_ref[0, 2:3, :]
    sh2 = mod_ref[0, 3:4, :]
    sc2 = mod_ref[0, 4:5, :]
    gt2 = mod_ref[0, 5:6, :]

    mix = (jnp.dot(oda_ref[0], wout_ref[0:DA_WIDTH, :], preferred_element_type=F32)
           + jnp.dot(oret_ref[0], wout_ref[DA_WIDTH:, :], preferred_element_type=F32))
    x1 = x_ref[0] + gt1 * (_rms(mix) * gpm_ref[...])
    h = ((_rms(x1) * gpf_ref[...]) * (1.0 + sc2) + sh2).astype(BF16)

    def conv(y, c0):
        cols = slice(c0, c0 + FF_CW)
        ybuf_ref[0:8, :] = carry_ref[:, cols]
        ybuf_ref[8:8 + tm, :] = y
        carry_ref[:, cols] = y[tm - 8:tm, :]
        y1 = ybuf_ref[7:7 + tm, :]
        y2 = ybuf_ref[6:6 + tm, :]
        return (cb_ref[:, cols] + cw_ref[0:1, cols] * y2 + cw_ref[1:2, cols] * y1
                + cw_ref[2:3, cols] * y)

    acc = jnp.zeros((tm, D_MODEL), F32)
    for c in range(D_FF // FF_CW):
        g0 = c * FF_CW
        v0 = D_FF + c * FF_CW
        ug = conv(jnp.dot(h, wup_ref[:, g0:g0 + FF_CW], preferred_element_type=F32), g0)
        uv = conv(jnp.dot(h, wup_ref[:, v0:v0 + FF_CW], preferred_element_type=F32), v0)
        f = (ug * jax.nn.sigmoid(ug) * uv).astype(BF16)
        acc = acc + jnp.dot(f, wdn_ref[g0:g0 + FF_CW, :], preferred_element_type=F32)

    o_ref[0] = x1 + gt2 * (_rms(acc) * gpo_ref[...])


def _ffn_call(o_da, o_ret, x, mod, g_post_mix, g_pre_ffn, g_post_ffn, w_out_b, w_up_b,
              conv_w, conv_b, w_down_b):
    b, s, d = x.shape
    tm = TM_FFN
    const = dict(pipeline_mode=pl.Buffered(1))
    vec = lambda n: pl.BlockSpec((1, n), lambda bi, i: (0, 0))
    return pl.pallas_call(
        _ffn_kernel,
        grid=(b, s // tm),
        in_specs=[pl.BlockSpec((1, tm, DA_WIDTH), lambda bi, i: (bi, i, 0)),
                  pl.BlockSpec((1, tm, RET_WIDTH), lambda bi, i: (bi, i, 0)),
                  pl.BlockSpec((1, tm, d), lambda bi, i: (bi, i, 0)),
                  pl.BlockSpec((1, N_MOD, d), lambda bi, i: (bi, 0, 0)),
                  vec(d), vec(d), vec(d),
                  pl.BlockSpec((d, d), lambda bi, i: (0, 0), **const),
                  pl.BlockSpec((d, 2 * D_FF), lambda bi, i: (0, 0), **const),
                  pl.BlockSpec((3, 2 * D_FF), lambda bi, i: (0, 0)),
                  pl.BlockSpec((1, 2 * D_FF), lambda bi, i: (0, 0)),
                  pl.BlockSpec((D_FF, d), lambda bi, i: (0, 0), **const)],
        out_specs=pl.BlockSpec((1, tm, d), lambda bi, i: (bi, i, 0)),
        out_shape=jax.ShapeDtypeStruct((b, s, d), F32),
        scratch_shapes=[pltpu.VMEM((tm + 8, FF_CW), F32),
                        pltpu.VMEM((8, 2 * D_FF), F32)],
        compiler_params=pltpu.CompilerParams(dimension_semantics=("arbitrary", "arbitrary"),
                                             vmem_limit_bytes=VMEM_LIMIT_BYTES),
        name="ffn",
    )(o_da, o_ret, x, mod, g_post_mix, g_pre_ffn, g_post_ffn, w_out_b, w_up_b,
      conv_w, conv_b, w_down_b)


def kernel(x, c, w_ada, b_ada, g_pre_mix, w_in, lam_q1, lam_k1, lam_q2, lam_k2, g_da_subln,
           w_out, g_post_mix, g_pre_ffn, w_up, conv_w, conv_b, w_down, g_post_ffn):
    b, s, d = x.shape
    depth = w_ada.shape[0]
    assert depth == 1 and d == D_MODEL and s % TM_IN == 0 and s % TQ == 0
    for l in range(depth):
        mod = _mod_call(c, w_ada[l], b_ada[l]).reshape(b, N_MOD, d)
        w_in_b = w_in[l].astype(BF16)
        w_vt_b = w_in[l][:, 1024:1536].T.astype(BF16)
        qk, vt, ret = _inproj_call(x, mod, g_pre_mix[l][None], w_in_b, w_vt_b)
        o_da = _dattn_call(qk, vt, lam_q1[l][None], lam_k1[l][None], lam_q2[l][None],
                           lam_k2[l][None], g_da_subln[l][None])
        o_ret = _ret_call(ret)
        x = _ffn_call(o_da, o_ret, x, mod, g_post_mix[l][None], g_pre_ffn[l][None],
                      g_post_ffn[l][None], w_out[l].astype(BF16), w_up[l].astype(BF16),
                      conv_w[l], conv_b[l][None], w_down[l].astype(BF16))
    return x
```

```python
import functools
import math

import jax
import jax.numpy as jnp
from jax import lax
from jax.experimental import pallas as pl
from jax.experimental.pallas import tpu as pltpu

D_MODEL = 1024
CHUNK = 64
DA_HEADS = 4
DA_QK = 64
DA_V = 128
DA_WIDTH = 512
RET_HEADS = 4
RET_QK = 64
RET_V = 128
RET_WIDTH = 512
IN_WIDTH = 3072
D_FF = 2816
N_MOD = 6
RMS_EPS = 1e-6
LAMBDA_INIT = 0.8 - 0.6 * math.exp(-0.3 * 0)

LOG2E = 1.4426950408889634
Q_FOLD = (DA_QK ** -0.5) * LOG2E
RET_K_FOLD = RET_QK ** -0.5

VMEM_LIMIT_BYTES = 56 * 1024 * 1024

BF16 = jnp.bfloat16
F32 = jnp.float32

TM_IN = 512
TQ = 256
TK = 256
RET_L = 256
TM_FFN = 512
FFN_SUB = 256
FF_CW = 256


def _rms(x):
    return x * lax.rsqrt(jnp.mean(x * x, axis=-1, keepdims=True) + RMS_EPS)


def _mod_kernel(c_ref, w_ref, b_ref, o_ref):
    c = c_ref[...]
    a = c * jax.nn.sigmoid(c)
    o_ref[...] = jnp.dot(a, w_ref[...], preferred_element_type=F32,
                         precision=lax.Precision.HIGHEST) + b_ref[...]


def _mod_call(c, w_ada, b_ada):
    b, d = c.shape
    n = w_ada.shape[1]
    tn = 1024
    return pl.pallas_call(
        _mod_kernel,
        grid=(n // tn,),
        in_specs=[pl.BlockSpec((b, d), lambda j: (0, 0)),
                  pl.BlockSpec((d, tn), lambda j: (0, j)),
                  pl.BlockSpec((1, tn), lambda j: (0, j))],
        out_specs=pl.BlockSpec((b, tn), lambda j: (0, j)),
        out_shape=jax.ShapeDtypeStruct((b, n), F32),
        compiler_params=pltpu.CompilerParams(dimension_semantics=("arbitrary",),
                                             vmem_limit_bytes=VMEM_LIMIT_BYTES),
        name="mod",
    )(c, w_ada, b_ada.reshape(1, n))


def _inproj_kernel(x_ref, mod_ref, g_ref, w_ref, wvt_ref, qk_ref, vt_ref, ret_ref):
    x = x_ref[0]
    sh = mod_ref[0, 0:1, :]
    sc = mod_ref[0, 1:2, :]
    h = (_rms(x) * g_ref[...]) * (1.0 + sc) + sh
    hb = h.astype(BF16)

    def proj(lo, hi):
        return jnp.dot(hb, w_ref[:, lo:hi], preferred_element_type=F32)

    qk_ref[0, :, 0:512] = (proj(0, 512) * Q_FOLD).astype(BF16)
    qk_ref[0, :, 512:1024] = proj(512, 1024).astype(BF16)
    vt = lax.dot_general(wvt_ref[...], hb, (((1,), (1,)), ((), ())),
                         preferred_element_type=F32).astype(BF16)
    for j in range(vt_ref.shape[1]):
        vt_ref[0, j] = vt[:, j * TK:(j + 1) * TK]
    ret_ref[0, :, 0:256] = proj(1536, 1792).astype(BF16)
    ret_ref[0, :, 256:512] = (proj(1792, 2048) * RET_K_FOLD).astype(BF16)
    ret_ref[0, :, 512:1024] = proj(2048, 2560).astype(BF16)
    ret_ref[0, :, 1024:1536] = proj(2560, 3072).astype(BF16)


def _inproj_call(x, mod, g_pre_mix, w_in_b, w_vt_b):
    b, s, d = x.shape
    tm = TM_IN
    const = dict(pipeline_mode=pl.Buffered(1))
    return pl.pallas_call(
        _inproj_kernel,
        grid=(b, s // tm),
        in_specs=[pl.BlockSpec((1, tm, d), lambda bi, i: (bi, i, 0)),
                  pl.BlockSpec((1, N_MOD, d), lambda bi, i: (bi, 0, 0)),
                  pl.BlockSpec((1, d), lambda bi, i: (0, 0)),
                  pl.BlockSpec((d, IN_WIDTH), lambda bi, i: (0, 0), **const),
                  pl.BlockSpec((DA_WIDTH, d), lambda bi, i: (0, 0), **const)],
        out_specs=[pl.BlockSpec((1, tm, 1024), lambda bi, i: (bi, i, 0)),
                   pl.BlockSpec((1, tm // TK, DA_WIDTH, TK), lambda bi, i: (bi, i, 0, 0)),
                   pl.BlockSpec((1, tm, 1536), lambda bi, i: (bi, i, 0))],
        out_shape=[jax.ShapeDtypeStruct((b, s, 1024), BF16),
                   jax.ShapeDtypeStruct((b, s // TK, DA_WIDTH, TK), BF16),
                   jax.ShapeDtypeStruct((b, s, 1536), BF16)],
        compiler_params=pltpu.CompilerParams(dimension_semantics=("arbitrary", "arbitrary"),
                                             vmem_limit_bytes=VMEM_LIMIT_BYTES),
        name="inproj",
    )(x, mod, g_pre_mix, w_in_b, w_vt_b)


def _dattn_kernel(q_ref, k_ref, vt_ref, lq1_ref, lk1_ref, lq2_ref, lk2_ref, gsub_ref,
                  o_ref, dbias_ref, cpos_ref, acc_ref, tbuf_ref, pbuf_ref):
    qi = pl.program_id(1)
    slopes = [LOG2E * 2.0 ** (-8.0 * (h + 1) / DA_HEADS) for h in range(DA_HEADS)]

    lane = lax.broadcasted_iota(jnp.int32, (TQ, 128), 1)

    @pl.when(qi == 0)
    def _():
        r = lax.broadcasted_iota(jnp.int32, (TK, 128), 0).astype(F32)
        kl = lax.broadcasted_iota(jnp.int32, (TK, 128), 1)
        c = lax.broadcasted_iota(jnp.int32, (TK, TQ), 0)
        a = lax.broadcasted_iota(jnp.int32, (TK, TQ), 1)
        rel = (a - jnp.abs(a - c)).astype(F32)
        allowed = (c // CHUNK) <= (a // CHUNK)
        for h in range(DA_HEADS):
            dbias_ref[h] = jnp.where(allowed, slopes[h] * rel, -jnp.inf)
            x = slopes[h] * r
            hi = x.astype(BF16).astype(F32)
            r1 = x - hi
            mid = r1.astype(BF16).astype(F32)
            lo = r1 - mid
            cpos_ref[h] = jnp.where(kl == 0, hi, jnp.where(kl == 1, mid,
                                    jnp.where(kl == 2, lo, 0.0))).astype(BF16)

    ones_lanes = jnp.where(lax.broadcasted_iota(jnp.int32, (2 * TQ, 128), 1) < 3,
                           1.0, 0.0).astype(BF16)
    ones_rows = jnp.ones((16, TK), BF16)
    qqs, qqa = [], []
    for h in range(DA_HEADS):
        q = q_ref[0, :, 128 * h:128 * h + 128]
        zero = jnp.zeros_like(q)
        qq = jnp.concatenate([jnp.where(lane < DA_QK, q, zero),
                              jnp.where(lane >= DA_QK, q, zero)], axis=0)
        qqs.append(qq)
        qqa.append(jnp.concatenate([qq, ones_lanes], axis=1))

    nt = (((1,), (1,)), ((), ()))

    def kblock(h, kb):
        return k_ref[0, pl.ds(pl.multiple_of(kb * TK, TK), TK), 128 * h:128 * h + 128]

    def diag_scores(h):
        db = dbias_ref[h]
        s = lax.dot_general(kblock(h, qi), qqs[h], nt, preferred_element_type=F32)
        return s + jnp.concatenate([db, db], axis=1)

    def past_scores(h, kb):
        lhs = jnp.concatenate([kblock(h, kb), cpos_ref[h]], axis=1)
        return lax.dot_general(lhs, qqa[h], nt, preferred_element_type=F32)

    def softmax(t, m, shift, tmax=None):
        mb = (jnp.max(t, axis=0, keepdims=True) if tmax is None else tmax) + shift
        m_new = jnp.maximum(m, mb)
        alpha = jnp.exp2(m - m_new)
        p = jnp.exp2(t - (m_new - shift))
        return p.astype(BF16), alpha, m_new

    def accumulate(h, kb, p, alpha, l):
        lhs = jnp.concatenate([vt_ref[0, kb, 128 * h:128 * h + 128, :], ones_rows], axis=0)
        pv = jnp.dot(lhs, p, preferred_element_type=F32)
        acc_ref[h] = alpha * acc_ref[h] + pv[0:DA_V]
        return alpha * l + pv[DA_V:DA_V + 1]


    ms, ls = [], []
    for h in range(DA_HEADS):
        acc_ref[h] = jnp.zeros((DA_V, 2 * TQ), F32)
        ms.append(jnp.full((1, 2 * TQ), -jnp.inf, F32))
        ls.append(jnp.zeros((1, 2 * TQ), F32))
    last = DA_HEADS - 1
    ts = [diag_scores(0), diag_scores(1)]
    for h in range(DA_HEADS):
        p, alpha, ms[h] = softmax(ts[h], ms[h], 0.0)
        if h + 2 < DA_HEADS:
            ts.append(diag_scores(h + 2))
        if h < last:
            ls[h] = accumulate(h, qi, p, alpha, ls[h])
        else:
            pbuf_ref[...] = p

    def start_next(h, kb):
        t = past_scores(h, kb)
        tbuf_ref[h] = t
        return jnp.max(t, axis=0, keepdims=True)

    mb0 = start_next(0, 0)
    mb1 = start_next(1, 0)

    def past(kb, carry):
        ms, ls = list(carry[:DA_HEADS]), list(carry[DA_HEADS:2 * DA_HEADS])
        alpha_pend, kb_pend, mb0, mb1 = carry[2 * DA_HEADS:]
        nxt = jnp.minimum(kb + 1, qi - 1)
        shifts = [slopes[h] * ((kb - qi) * TK).astype(F32) for h in range(DA_HEADS)]
        ls[3] = accumulate(3, kb_pend, pbuf_ref[...], alpha_pend, ls[3])
        p0, a0, ms[0] = softmax(tbuf_ref[0], ms[0], shifts[0], mb0)
        t2 = past_scores(2, kb)
        ls[0] = accumulate(0, kb, p0, a0, ls[0])
        p1, a1, ms[1] = softmax(tbuf_ref[1], ms[1], shifts[1], mb1)
        t3 = past_scores(3, kb)
        mb0 = start_next(0, nxt)
        ls[1] = accumulate(1, kb, p1, a1, ls[1])
        p2, a2, ms[2] = softmax(t2, ms[2], shifts[2])
        mb1 = start_next(1, nxt)
        ls[2] = accumulate(2, kb, p2, a2, ls[2])
        p3, a3, ms[3] = softmax(t3, ms[3], shifts[3])
        pbuf_ref[...] = p3
        return tuple(ms) + tuple(ls) + (a3, kb, mb0, mb1)

    carry = lax.fori_loop(0, qi, past, tuple(ms) + tuple(ls) + (alpha, qi, mb0, mb1))
    ls = list(carry[DA_HEADS:2 * DA_HEADS])
    ls[last] = accumulate(last, carry[2 * DA_HEADS + 1], pbuf_ref[...], carry[2 * DA_HEADS],
                          ls[last])
    carry = [x for h in range(DA_HEADS) for x in (carry[h], ls[h])]

    lam = (jnp.exp(jnp.sum(lq1_ref[...] * lk1_ref[...], axis=-1, keepdims=True))
           - jnp.exp(jnp.sum(lq2_ref[...] * lk2_ref[...], axis=-1, keepdims=True))
           + LAMBDA_INIT)
    for h in range(DA_HEADS):
        inv = 1.0 / carry[2 * h + 1]
        acc = acc_ref[h]
        ot = acc[:, :TQ] * inv[:, :TQ] - lam * (acc[:, TQ:] * inv[:, TQ:])
        ot = ot * lax.rsqrt(jnp.mean(ot * ot, axis=0, keepdims=True) + RMS_EPS)
        o = ot.T * gsub_ref[...] * (1.0 - LAMBDA_INIT)
        o_ref[0, :, 128 * h:128 * h + 128] = o.astype(BF16)


def _dattn_call(qk, vt, lam_q1, lam_k1, lam_q2, lam_k2, g_sub):
    b, s, _ = qk.shape
    nkv = s // TK
    vec = lambda n: pl.BlockSpec((1, n), lambda bi, i: (0, 0))
    return pl.pallas_call(
        _dattn_kernel,
        grid=(b, s // TQ),
        in_specs=[pl.BlockSpec((1, TQ, DA_WIDTH), lambda bi, i: (bi, i, 0)),
                  pl.BlockSpec((1, s, DA_WIDTH), lambda bi, i: (bi, 0, 1)),
                  pl.BlockSpec((1, nkv, DA_WIDTH, TK), lambda bi, i: (bi, 0, 0, 0)),
                  vec(DA_QK), vec(DA_QK), vec(DA_QK), vec(DA_QK), vec(DA_V)],
        out_specs=pl.BlockSpec((1, TQ, DA_WIDTH), lambda bi, i: (bi, i, 0)),
        out_shape=jax.ShapeDtypeStruct((b, s, DA_WIDTH), BF16),
        scratch_shapes=[pltpu.VMEM((DA_HEADS, TK, TQ), F32),
                        pltpu.VMEM((DA_HEADS, TK, 128), BF16),
                        pltpu.VMEM((DA_HEADS, DA_V, 2 * TQ), F32),
                        pltpu.VMEM((2, TK, 2 * TQ), F32),
                        pltpu.VMEM((TK, 2 * TQ), BF16)],
        compiler_params=pltpu.CompilerParams(
            dimension_semantics=("arbitrary", "arbitrary"),
            vmem_limit_bytes=VMEM_LIMIT_BYTES),
        name="dattn",
    )(qk, qk, vt, lam_q1, lam_k1, lam_q2, lam_k2, g_sub)


def _ret_kernel(q_ref, k_ref, v_ref, g_ref, o_ref, dec_ref, state_ref):
    i = pl.program_id(1)
    L = RET_L
    row = lax.broadcasted_iota(jnp.int32, (L, L), 0)
    col = lax.broadcasted_iota(jnp.int32, (L, L), 1)

    @pl.when(i == 0)
    def _():
        state_ref[...] = jnp.zeros_like(state_ref)
        d = row - col
        same = (row // CHUNK) == (col // CHUNK)
        past = (col // CHUNK) < (row // CHUNK)
        e = jnp.where(same, jnp.abs(d), d).astype(F32)
        for h in range(RET_HEADS):
            lg = math.log(1.0 - 2.0 ** (-5.0 - h))
            dec_ref[h] = jnp.where(same | past, jnp.exp(lg * e), 0.0)

    lane = lax.broadcasted_iota(jnp.int32, (L, 128), 1)
    r1 = lax.broadcasted_iota(jnp.int32, (L, 1), 0).astype(F32)
    for h in range(RET_HEADS):
        lg = math.log(1.0 - 2.0 ** (-5.0 - h))
        pair = slice(128 * (h // 2), 128 * (h // 2) + 128)
        lo = RET_QK * (h % 2)
        sel = (lane >= lo) & (lane < lo + RET_QK)
        qp = q_ref[0, :, pair]
        kp = k_ref[0, :, pair]
        qz = jnp.where(sel, qp, jnp.zeros_like(qp))
        kz = jnp.where(sel, kp, jnp.zeros_like(kp))
        vh = v_ref[0, :, 128 * h:128 * h + 128]
        s = lax.dot_general(qz, kz, (((1,), (1,)), ((), ())), preferred_element_type=F32)
        s = s * dec_ref[h]
        o = jnp.dot(s.astype(BF16), vh, preferred_element_type=F32)
        st = state_ref[h]
        out_dec = jnp.exp(lg * (r1 + 1.0))
        o = o + out_dec * jnp.dot(qz, st.astype(BF16), preferred_element_type=F32)
        in_dec = jnp.exp(lg * (L - 1.0 - r1))
        kd = (kz.astype(F32) * in_dec).astype(BF16)
        u = lax.dot_general(kd, vh, (((0,), (0,)), ((), ())), preferred_element_type=F32)
        state_ref[h] = math.exp(lg * L) * st + u
        g = g_ref[0, :, 128 * h:128 * h + 128].astype(F32)
        o = _rms(o) * (g * jax.nn.sigmoid(g))
        o_ref[0, :, 128 * h:128 * h + 128] = o.astype(BF16)


def _ret_call(ret):
    b, s, _ = ret.shape
    L = RET_L
    return pl.pallas_call(
        _ret_kernel,
        grid=(b, s // L),
        in_specs=[pl.BlockSpec((1, L, 256), lambda bi, i: (bi, i, 0)),
                  pl.BlockSpec((1, L, 256), lambda bi, i: (bi, i, 1)),
                  pl.BlockSpec((1, L, 512), lambda bi, i: (bi, i, 1)),
                  pl.BlockSpec((1, L, 512), lambda bi, i: (bi, i, 2))],
        out_specs=pl.BlockSpec((1, L, RET_WIDTH), lambda bi, i: (bi, i, 0)),
        out_shape=jax.ShapeDtypeStruct((b, s, RET_WIDTH), BF16),
        scratch_shapes=[pltpu.VMEM((RET_HEADS, L, L), F32),
                        pltpu.VMEM((RET_HEADS, 128, RET_V), F32)],
        compiler_params=pltpu.CompilerParams(dimension_semantics=("arbitrary", "arbitrary"),
                                             vmem_limit_bytes=VMEM_LIMIT_BYTES),
        name="ret",
    )(ret, ret, ret, ret)


def _ffn_kernel(oda_ref, oret_ref, x_ref, mod_ref, gpm_ref, gpf_ref, gpo_ref,
                wout_ref, wup_ref, cw_ref, cb_ref, wdn_ref, o_ref, perm_ref, f_ref, carry_ref):
    i = pl.program_id(1)
    ts = FFN_SUB
    n_sub = x_ref.shape[1] // ts
    n_groups = ts // 64

    @pl.when(i == 0)
    def _():
        carry_ref[...] = jnp.zeros_like(carry_ref)

    def permute_rows(a, slot):
        n = a.shape[1] // 128
        for s in range(n):
            perm_ref[slot, s] = a[:, 128 * s:128 * s + 128]
        rows = []
        for g in range(n_groups):
            for r in range(8):
                rows.append(jnp.concatenate(
                    [perm_ref[slot, s, pl.ds(64 * g + r, 8, stride=8), :] for s in range(n)],
                    axis=1))
        return jnp.concatenate(rows, axis=0)

    gt1 = mod_ref[0, 2:3, :]
    sh2 = mod_ref[0, 3:4, :]
    sc2 = mod_ref[0, 4:5, :]
    gt2 = mod_ref[0, 5:6, :]
    sub0 = lax.broadcasted_iota(jnp.int32, (8, FF_CW), 0) == 0
    n_chunks = D_FF // FF_CW

    def pre(t):
        rows = slice(t * ts, (t + 1) * ts)
        mix = (jnp.dot(oda_ref[0, rows, :], wout_ref[0:DA_WIDTH, :], preferred_element_type=F32)
               + jnp.dot(oret_ref[0, rows, :], wout_ref[DA_WIDTH:, :],
                         preferred_element_type=F32))
        x1 = x_ref[0, rows, :] + gt1 * (_rms(mix) * gpm_ref[...])
        o_ref[0, rows, :] = x1
        return permute_rows((_rms(x1) * gpf_ref[...]) * (1.0 + sc2) + sh2, 2 * t).astype(BF16)

    def conv(y, c0):
        cols = slice(c0, c0 + FF_CW)
        w0, w1, w2, cb = cw_ref[0:1, cols], cw_ref[1:2, cols], cw_ref[2:3, cols], cb_ref[:, cols]
        prev7 = carry_ref[0, :, cols]
        prev6 = carry_ref[1, :, cols]
        out = []
        for g in range(n_groups):
            yg = y[64 * g:64 * g + 64]
            rot7 = pltpu.roll(yg[56:64], 1, 0)
            rot6 = pltpu.roll(yg[48:56], 1, 0)
            sp1 = jnp.where(sub0, prev7, rot7)
            sp2 = jnp.where(sub0, prev6, rot6)
            y1 = jnp.concatenate([sp1, yg[0:56]], axis=0)
            y2 = jnp.concatenate([sp2, sp1, yg[0:48]], axis=0)
            out.append(cb + w0 * y2 + w1 * y1 + w2 * yg)
            prev7, prev6 = rot7, rot6
        carry_ref[0, :, cols] = prev7
        carry_ref[1, :, cols] = prev6
        return jnp.concatenate(out, axis=0)

    def mlp(t, h):
        def up(c):
            return [jnp.dot(h, wup_ref[:, c0:c0 + FF_CW], preferred_element_type=F32)
                    for c0 in (c * FF_CW, D_FF + c * FF_CW)]
        ys = up(0)
        for c in range(n_chunks):
            nxt = up(c + 1) if c + 1 < n_chunks else None
            ug = conv(ys[0], c * FF_CW)
            uv = conv(ys[1], D_FF + c * FF_CW)
            f_ref[t, :, c * FF_CW:(c + 1) * FF_CW] = (ug * jax.nn.sigmoid(ug) * uv).astype(BF16)
            ys = nxt
        return jnp.dot(f_ref[t], wdn_ref[...], preferred_element_type=F32)

    def post(t, acc):
        rows = slice(t * ts, (t + 1) * ts)
        o_ref[0, rows, :] = (o_ref[0, rows, :]
                             + gt2 * (permute_rows(_rms(acc), 2 * t + 1) * gpo_ref[...]))

    hs = [pre(0)]
    for t in range(n_sub):
        if t + 1 < n_sub:
            hs.append(pre(t + 1))
        acc = mlp(t, hs[t])
        post(t, acc)


def _ffn_call(o_da, o_ret, x, mod, g_post_mix, g_pre_ffn, g_post_ffn, w_out_b, w_up_b,
              conv_w, conv_b, w_down_b):
    b, s, d = x.shape
    tm = TM_FFN
    const = dict(pipeline_mode=pl.Buffered(1))
    vec = lambda n: pl.BlockSpec((1, n), lambda bi, i: (0, 0))
    return pl.pallas_call(
        _ffn_kernel,
        grid=(b, s // tm),
        in_specs=[pl.BlockSpec((1, tm, DA_WIDTH), lambda bi, i: (bi, i, 0)),
                  pl.BlockSpec((1, tm, RET_WIDTH), lambda bi, i: (bi, i, 0)),
                  pl.BlockSpec((1, tm, d), lambda bi, i: (bi, i, 0)),
                  pl.BlockSpec((1, N_MOD, d), lambda bi, i: (bi, 0, 0)),
                  vec(d), vec(d), vec(d),
                  pl.BlockSpec((d, d), lambda bi, i: (0, 0), **const),
                  pl.BlockSpec((d, 2 * D_FF), lambda bi, i: (0, 0), **const),
                  pl.BlockSpec((3, 2 * D_FF), lambda bi, i: (0, 0)),
                  pl.BlockSpec((1, 2 * D_FF), lambda bi, i: (0, 0)),
                  pl.BlockSpec((D_FF, d), lambda bi, i: (0, 0), **const)],
        out_specs=pl.BlockSpec((1, tm, d), lambda bi, i: (bi, i, 0)),
        out_shape=jax.ShapeDtypeStruct((b, s, d), F32),
        scratch_shapes=[pltpu.VMEM((2 * (tm // FFN_SUB), d // 128, FFN_SUB, 128), F32),
                        pltpu.VMEM((tm // FFN_SUB, FFN_SUB, D_FF), BF16),
                        pltpu.VMEM((2, 8, 2 * D_FF), F32)],
        compiler_params=pltpu.CompilerParams(dimension_semantics=("arbitrary", "arbitrary"),
                                             vmem_limit_bytes=VMEM_LIMIT_BYTES),
        name="ffn",
    )(o_da, o_ret, x, mod, g_post_mix, g_pre_ffn, g_post_ffn, w_out_b, w_up_b,
      conv_w, conv_b, w_down_b)


def kernel(x, c, w_ada, b_ada, g_pre_mix, w_in, lam_q1, lam_k1, lam_q2, lam_k2, g_da_subln,
           w_out, g_post_mix, g_pre_ffn, w_up, conv_w, conv_b, w_down, g_post_ffn):
    b, s, d = x.shape
    depth = w_ada.shape[0]
    assert depth == 1 and d == D_MODEL and s % TM_IN == 0 and s % TQ == 0
    for l in range(depth):
        mod = _mod_call(c, w_ada[l], b_ada[l]).reshape(b, N_MOD, d)
        w_in_b = w_in[l].astype(BF16)
        w_vt_b = w_in[l][:, 1024:1536].T.astype(BF16)
        qk, vt, ret = _inproj_call(x, mod, g_pre_mix[l][None], w_in_b, w_vt_b)
        o_da = _dattn_call(qk, vt, lam_q1[l][None], lam_k1[l][None], lam_q2[l][None],
                           lam_k2[l][None], g_da_subln[l][None])
        o_ret = _ret_call(ret)
        x = _ffn_call(o_da, o_ret, x, mod, g_post_mix[l][None], g_pre_ffn[l][None],
                      g_post_ffn[l][None], w_out[l].astype(BF16), w_up[l].astype(BF16),
                      conv_w[l], conv_b[l][None], w_down[l].astype(BF16))
    return x
```

```python
import functools
import math

import jax
import jax.numpy as jnp
from jax import lax
from jax.experimental import pallas as pl
from jax.experimental.pallas import tpu as pltpu

D_MODEL = 1024
CHUNK = 64
DA_HEADS = 4
DA_QK = 64
DA_V = 128
DA_WIDTH = 512
RET_HEADS = 4
RET_QK = 64
RET_V = 128
RET_WIDTH = 512
IN_WIDTH = 3072
D_FF = 2816
N_MOD = 6
RMS_EPS = 1e-6
LAMBDA_INIT = 0.8 - 0.6 * math.exp(-0.3 * 0)

LOG2E = 1.4426950408889634
Q_FOLD = (DA_QK ** -0.5) * LOG2E
RET_K_FOLD = RET_QK ** -0.5

VMEM_LIMIT_BYTES = 56 * 1024 * 1024

BF16 = jnp.bfloat16
F32 = jnp.float32

TM_IN = 512
TQ = 256
TK = 256
RET_L = 256
TM_FFN = 512
FFN_SUB = 256
FF_CW = 256


def _rms(x):
    return x * lax.rsqrt(jnp.mean(x * x, axis=-1, keepdims=True) + RMS_EPS)


def _mod_kernel(c_ref, w_ref, b_ref, o_ref):
    c = c_ref[...]
    a = c * jax.nn.sigmoid(c)
    o_ref[...] = jnp.dot(a, w_ref[...], preferred_element_type=F32,
                         precision=lax.Precision.HIGHEST) + b_ref[...]


def _mod_call(c, w_ada, b_ada):
    b, d = c.shape
    n = w_ada.shape[1]
    tn = 1024
    return pl.pallas_call(
        _mod_kernel,
        grid=(n // tn,),
        in_specs=[pl.BlockSpec((b, d), lambda j: (0, 0)),
                  pl.BlockSpec((d, tn), lambda j: (0, j)),
                  pl.BlockSpec((1, tn), lambda j: (0, j))],
        out_specs=pl.BlockSpec((b, tn), lambda j: (0, j)),
        out_shape=jax.ShapeDtypeStruct((b, n), F32),
        compiler_params=pltpu.CompilerParams(dimension_semantics=("arbitrary",),
                                             vmem_limit_bytes=VMEM_LIMIT_BYTES),
        name="mod",
    )(c, w_ada, b_ada.reshape(1, n))


def _inproj_kernel(x_ref, mod_ref, g_ref, w_ref, wvt_ref, qk_ref, vt_ref, ret_ref):
    x = x_ref[0]
    sh = mod_ref[0, 0:1, :]
    sc = mod_ref[0, 1:2, :]
    h = (_rms(x) * g_ref[...]) * (1.0 + sc) + sh
    hb = h.astype(BF16)

    def proj(lo, hi):
        return jnp.dot(hb, w_ref[:, lo:hi], preferred_element_type=F32)

    qk_ref[0, :, 0:512] = (proj(0, 512) * Q_FOLD).astype(BF16)
    qk_ref[0, :, 512:1024] = proj(512, 1024).astype(BF16)
    vt = lax.dot_general(wvt_ref[...], hb, (((1,), (1,)), ((), ())),
                         preferred_element_type=F32).astype(BF16)
    for j in range(vt_ref.shape[1]):
        vt_ref[0, j] = vt[:, j * TK:(j + 1) * TK]
    ret_ref[0, :, 0:256] = proj(1536, 1792).astype(BF16)
    ret_ref[0, :, 256:512] = (proj(1792, 2048) * RET_K_FOLD).astype(BF16)
    ret_ref[0, :, 512:1024] = proj(2048, 2560).astype(BF16)
    ret_ref[0, :, 1024:1536] = proj(2560, 3072).astype(BF16)


def _inproj_call(x, mod, g_pre_mix, w_in_b, w_vt_b):
    b, s, d = x.shape
    tm = TM_IN
    const = dict(pipeline_mode=pl.Buffered(1))
    return pl.pallas_call(
        _inproj_kernel,
        grid=(b, s // tm),
        in_specs=[pl.BlockSpec((1, tm, d), lambda bi, i: (bi, i, 0)),
                  pl.BlockSpec((1, N_MOD, d), lambda bi, i: (bi, 0, 0)),
                  pl.BlockSpec((1, d), lambda bi, i: (0, 0)),
                  pl.BlockSpec((d, IN_WIDTH), lambda bi, i: (0, 0), **const),
                  pl.BlockSpec((DA_WIDTH, d), lambda bi, i: (0, 0), **const)],
        out_specs=[pl.BlockSpec((1, tm, 1024), lambda bi, i: (bi, i, 0)),
                   pl.BlockSpec((1, tm // TK, DA_WIDTH, TK), lambda bi, i: (bi, i, 0, 0)),
                   pl.BlockSpec((1, tm, 1536), lambda bi, i: (bi, i, 0))],
        out_shape=[jax.ShapeDtypeStruct((b, s, 1024), BF16),
                   jax.ShapeDtypeStruct((b, s // TK, DA_WIDTH, TK), BF16),
                   jax.ShapeDtypeStruct((b, s, 1536), BF16)],
        compiler_params=pltpu.CompilerParams(dimension_semantics=("arbitrary", "arbitrary"),
                                             vmem_limit_bytes=VMEM_LIMIT_BYTES),
        name="inproj",
    )(x, mod, g_pre_mix, w_in_b, w_vt_b)


def _dattn_kernel(q_ref, k_ref, vt_ref, lq1_ref, lk1_ref, lq2_ref, lk2_ref, gsub_ref,
                  o_ref, dbias_ref, cpos_ref, acc_ref, *bufs):
    tbufs, pbufs = bufs[:DA_HEADS], bufs[DA_HEADS:]
    qi = pl.program_id(1)
    slopes = [LOG2E * 2.0 ** (-8.0 * (h + 1) / DA_HEADS) for h in range(DA_HEADS)]

    lane = lax.broadcasted_iota(jnp.int32, (TQ, 128), 1)

    @pl.when(qi == 0)
    def _():
        r = lax.broadcasted_iota(jnp.int32, (TK, 128), 0).astype(F32)
        kl = lax.broadcasted_iota(jnp.int32, (TK, 128), 1)
        c = lax.broadcasted_iota(jnp.int32, (TK, TQ), 0)
        a = lax.broadcasted_iota(jnp.int32, (TK, TQ), 1)
        rel = (a - jnp.abs(a - c)).astype(F32)
        allowed = (c // CHUNK) <= (a // CHUNK)
        for h in range(DA_HEADS):
            dbias_ref[h] = jnp.where(allowed, slopes[h] * rel, -jnp.inf)
            x = slopes[h] * r
            hi = x.astype(BF16).astype(F32)
            r1 = x - hi
            mid = r1.astype(BF16).astype(F32)
            lo = r1 - mid
            cpos_ref[h] = jnp.where(kl == 0, hi, jnp.where(kl == 1, mid,
                                    jnp.where(kl == 2, lo, 0.0))).astype(BF16)

    ones_lanes = jnp.where(lax.broadcasted_iota(jnp.int32, (2 * TQ, 128), 1) < 3,
                           1.0, 0.0).astype(BF16)
    ones_rows = jnp.ones((16, TK), BF16)
    qqs, qqa = [], []
    for h in range(DA_HEADS):
        q = q_ref[0, :, 128 * h:128 * h + 128]
        zero = jnp.zeros_like(q)
        qq = jnp.concatenate([jnp.where(lane < DA_QK, q, zero),
                              jnp.where(lane >= DA_QK, q, zero)], axis=0)
        qqs.append(qq)
        qqa.append(jnp.concatenate([qq, ones_lanes], axis=1))

    nt = (((1,), (1,)), ((), ()))

    def kblock(h, kb):
        return k_ref[0, pl.ds(pl.multiple_of(kb * TK, TK), TK), 128 * h:128 * h + 128]

    def diag_scores(h):
        db = dbias_ref[h]
        s = lax.dot_general(kblock(h, qi), qqs[h], nt, preferred_element_type=F32)
        return s + jnp.concatenate([db, db], axis=1)

    def past_scores(h, kb):
        lhs = jnp.concatenate([kblock(h, kb), cpos_ref[h]], axis=1)
        return lax.dot_general(lhs, qqa[h], nt, preferred_element_type=F32)

    def score_stage(h, t):
        tbufs[h][...] = t
        return jnp.max(t, axis=0, keepdims=True)

    def softmax_stage(h, tmax, m, shift):
        m_new = jnp.maximum(m, tmax + shift)
        alpha = jnp.exp2(m - m_new)
        pbufs[h][...] = jnp.exp2(tbufs[h][...] - (m_new - shift)).astype(BF16)
        return alpha, m_new

    def value_stage(h, kb, alpha, l):
        lhs = jnp.concatenate([vt_ref[0, kb, 128 * h:128 * h + 128, :], ones_rows], axis=0)
        pv = jnp.dot(lhs, pbufs[h][...], preferred_element_type=F32)
        acc_ref[h] = alpha * acc_ref[h] + pv[0:DA_V]
        return alpha * l + pv[DA_V:DA_V + 1]

    heads = range(DA_HEADS)
    for h in heads:
        pbufs[h][...] = jnp.zeros((TK, 2 * TQ), BF16)
        acc_ref[h] = jnp.zeros((DA_V, 2 * TQ), F32)
    ms = [jnp.full((1, 2 * TQ), -jnp.inf, F32) for _ in heads]
    ls = [jnp.zeros((1, 2 * TQ), F32) for _ in heads]
    alphas = [jnp.ones((1, 2 * TQ), F32) for _ in heads]
    tmaxs = [score_stage(h, diag_scores(h)) for h in heads]

    def step(n, carry):
        ms, ls, alphas, tmaxs = (list(carry[4 * j:4 * j + 4]) for j in range(4))
        kb_value = jnp.where(n == 1, qi, jnp.maximum(n - 2, 0))
        kb_score = jnp.minimum(n, jnp.maximum(qi - 1, 0))
        rel = jnp.where(n == 0, 0, (n - 1 - qi) * TK).astype(F32)
        for h in heads:
            ls[h] = value_stage(h, kb_value, alphas[h], ls[h])
        for h in heads:
            alphas[h], ms[h] = softmax_stage(h, tmaxs[h], ms[h], slopes[h] * rel)
            tmaxs[h] = score_stage(h, past_scores(h, kb_score))
        return tuple(ms) + tuple(ls) + tuple(alphas) + tuple(tmaxs)

    carry = lax.fori_loop(0, qi + 1, step, tuple(ms) + tuple(ls) + tuple(alphas) + tuple(tmaxs))
    ms, ls, alphas = (list(carry[4 * j:4 * j + 4]) for j in range(3))
    kb_last = jnp.maximum(qi - 1, 0)
    for h in heads:
        ls[h] = value_stage(h, kb_last, alphas[h], ls[h])
    carry = [x for h in heads for x in (ms[h], ls[h])]

    lam = (jnp.exp(jnp.sum(lq1_ref[...] * lk1_ref[...], axis=-1, keepdims=True))
           - jnp.exp(jnp.sum(lq2_ref[...] * lk2_ref[...], axis=-1, keepdims=True))
           + LAMBDA_INIT)
    for h in range(DA_HEADS):
        inv = 1.0 / carry[2 * h + 1]
        acc = acc_ref[h]
        ot = acc[:, :TQ] * inv[:, :TQ] - lam * (acc[:, TQ:] * inv[:, TQ:])
        ot = ot * lax.rsqrt(jnp.mean(ot * ot, axis=0, keepdims=True) + RMS_EPS)
        o = ot.T * gsub_ref[...] * (1.0 - LAMBDA_INIT)
        o_ref[0, :, 128 * h:128 * h + 128] = o.astype(BF16)


def _dattn_call(qk, vt, lam_q1, lam_k1, lam_q2, lam_k2, g_sub):
    b, s, _ = qk.shape
    nkv = s // TK
    vec = lambda n: pl.BlockSpec((1, n), lambda bi, i: (0, 0))
    return pl.pallas_call(
        _dattn_kernel,
        grid=(b, s // TQ),
        in_specs=[pl.BlockSpec((1, TQ, DA_WIDTH), lambda bi, i: (bi, i, 0)),
                  pl.BlockSpec((1, s, DA_WIDTH), lambda bi, i: (bi, 0, 1)),
                  pl.BlockSpec((1, nkv, DA_WIDTH, TK), lambda bi, i: (bi, 0, 0, 0)),
                  vec(DA_QK), vec(DA_QK), vec(DA_QK), vec(DA_QK), vec(DA_V)],
        out_specs=pl.BlockSpec((1, TQ, DA_WIDTH), lambda bi, i: (bi, i, 0)),
        out_shape=jax.ShapeDtypeStruct((b, s, DA_WIDTH), BF16),
        scratch_shapes=[pltpu.VMEM((DA_HEADS, TK, TQ), F32),
                        pltpu.VMEM((DA_HEADS, TK, 128), BF16),
                        pltpu.VMEM((DA_HEADS, DA_V, 2 * TQ), F32)]
                       + [pltpu.VMEM((TK, 2 * TQ), F32) for _ in range(DA_HEADS)]
                       + [pltpu.VMEM((TK, 2 * TQ), BF16) for _ in range(DA_HEADS)],
        compiler_params=pltpu.CompilerParams(
            dimension_semantics=("arbitrary", "arbitrary"),
            vmem_limit_bytes=VMEM_LIMIT_BYTES),
        name="dattn",
    )(qk, qk, vt, lam_q1, lam_k1, lam_q2, lam_k2, g_sub)


def _ret_kernel(q_ref, k_ref, v_ref, g_ref, o_ref, dec_ref, state_ref):
    i = pl.program_id(1)
    L = RET_L
    row = lax.broadcasted_iota(jnp.int32, (L, L), 0)
    col = lax.broadcasted_iota(jnp.int32, (L, L), 1)

    @pl.when(i == 0)
    def _():
        state_ref[...] = jnp.zeros_like(state_ref)
        d = row - col
        same = (row // CHUNK) == (col // CHUNK)
        past = (col // CHUNK) < (row // CHUNK)
        e = jnp.where(same, jnp.abs(d), d).astype(F32)
        for h in range(RET_HEADS):
            lg = math.log(1.0 - 2.0 ** (-5.0 - h))
            dec_ref[h] = jnp.where(same | past, jnp.exp(lg * e), 0.0)

    lane = lax.broadcasted_iota(jnp.int32, (L, 128), 1)
    r1 = lax.broadcasted_iota(jnp.int32, (L, 1), 0).astype(F32)
    for h in range(RET_HEADS):
        lg = math.log(1.0 - 2.0 ** (-5.0 - h))
        pair = slice(128 * (h // 2), 128 * (h // 2) + 128)
        lo = RET_QK * (h % 2)
        sel = (lane >= lo) & (lane < lo + RET_QK)
        qp = q_ref[0, :, pair]
        kp = k_ref[0, :, pair]
        qz = jnp.where(sel, qp, jnp.zeros_like(qp))
        kz = jnp.where(sel, kp, jnp.zeros_like(kp))
        vh = v_ref[0, :, 128 * h:128 * h + 128]
        s = lax.dot_general(qz, kz, (((1,), (1,)), ((), ())), preferred_element_type=F32)
        s = s * dec_ref[h]
        o = jnp.dot(s.astype(BF16), vh, preferred_element_type=F32)
        st = state_ref[h]
        out_dec = jnp.exp(lg * (r1 + 1.0))
        o = o + out_dec * jnp.dot(qz, st.astype(BF16), preferred_element_type=F32)
        in_dec = jnp.exp(lg * (L - 1.0 - r1))
        kd = (kz.astype(F32) * in_dec).astype(BF16)
        u = lax.dot_general(kd, vh, (((0,), (0,)), ((), ())), preferred_element_type=F32)
        state_ref[h] = math.exp(lg * L) * st + u
        g = g_ref[0, :, 128 * h:128 * h + 128].astype(F32)
        o = _rms(o) * (g * jax.nn.sigmoid(g))
        o_ref[0, :, 128 * h:128 * h + 128] = o.astype(BF16)


def _ret_call(ret):
    b, s, _ = ret.shape
    L = RET_L
    return pl.pallas_call(
        _ret_kernel,
        grid=(b, s // L),
        in_specs=[pl.BlockSpec((1, L, 256), lambda bi, i: (bi, i, 0)),
                  pl.BlockSpec((1, L, 256), lambda bi, i: (bi, i, 1)),
                  pl.BlockSpec((1, L, 512), lambda bi, i: (bi, i, 1)),
                  pl.BlockSpec((1, L, 512), lambda bi, i: (bi, i, 2))],
        out_specs=pl.BlockSpec((1, L, RET_WIDTH), lambda bi, i: (bi, i, 0)),
        out_shape=jax.ShapeDtypeStruct((b, s, RET_WIDTH), BF16),
        scratch_shapes=[pltpu.VMEM((RET_HEADS, L, L), F32),
                        pltpu.VMEM((RET_HEADS, 128, RET_V), F32)],
        compiler_params=pltpu.CompilerParams(dimension_semantics=("arbitrary", "arbitrary"),
                                             vmem_limit_bytes=VMEM_LIMIT_BYTES),
        name="ret",
    )(ret, ret, ret, ret)


def _ffn_kernel(oda_ref, oret_ref, x_ref, mod_ref, gpm_ref, gpf_ref, gpo_ref,
                wout_ref, wup_ref, cw_ref, cb_ref, wdn_ref, o_ref, perm_ref, f_ref, carry_ref):
    i = pl.program_id(1)
    ts = FFN_SUB
    n_sub = x_ref.shape[1] // ts
    n_groups = ts // 64

    @pl.when(i == 0)
    def _():
        carry_ref[...] = jnp.zeros_like(carry_ref)

    def permute_rows(a, slot):
        n = a.shape[1] // 128
        for s in range(n):
            perm_ref[slot, s] = a[:, 128 * s:128 * s + 128]
        rows = []
        for g in range(n_groups):
            for r in range(8):
                rows.append(jnp.concatenate(
                    [perm_ref[slot, s, pl.ds(64 * g + r, 8, stride=8), :] for s in range(n)],
                    axis=1))
        return jnp.concatenate(rows, axis=0)

    gt1 = mod_ref[0, 2:3, :]
    sh2 = mod_ref[0, 3:4, :]
    sc2 = mod_ref[0, 4:5, :]
    gt2 = mod_ref[0, 5:6, :]
    sub0 = lax.broadcasted_iota(jnp.int32, (8, FF_CW), 0) == 0
    n_chunks = D_FF // FF_CW

    def pre(t):
        rows = slice(t * ts, (t + 1) * ts)
        mix = (jnp.dot(oda_ref[0, rows, :], wout_ref[0:DA_WIDTH, :], preferred_element_type=F32)
               + jnp.dot(oret_ref[0, rows, :], wout_ref[DA_WIDTH:, :],
                         preferred_element_type=F32))
        x1 = x_ref[0, rows, :] + gt1 * (_rms(mix) * gpm_ref[...])
        o_ref[0, rows, :] = x1
        return permute_rows((_rms(x1) * gpf_ref[...]) * (1.0 + sc2) + sh2, 2 * t).astype(BF16)

    def conv(y, c0):
        cols = slice(c0, c0 + FF_CW)
        w0, w1, w2, cb = cw_ref[0:1, cols], cw_ref[1:2, cols], cw_ref[2:3, cols], cb_ref[:, cols]
        prev7 = carry_ref[0, :, cols]
        prev6 = carry_ref[1, :, cols]
        out = []
        for g in range(n_groups):
            yg = y[64 * g:64 * g + 64]
            rot7 = pltpu.roll(yg[56:64], 1, 0)
            rot6 = pltpu.roll(yg[48:56], 1, 0)
            sp1 = jnp.where(sub0, prev7, rot7)
            sp2 = jnp.where(sub0, prev6, rot6)
            y1 = jnp.concatenate([sp1, yg[0:56]], axis=0)
            y2 = jnp.concatenate([sp2, sp1, yg[0:48]], axis=0)
            out.append(cb + w0 * y2 + w1 * y1 + w2 * yg)
            prev7, prev6 = rot7, rot6
        carry_ref[0, :, cols] = prev7
        carry_ref[1, :, cols] = prev6
        return jnp.concatenate(out, axis=0)

    def mlp(t, h):
        def up(c):
            return [jnp.dot(h, wup_ref[:, c0:c0 + FF_CW], preferred_element_type=F32)
                    for c0 in (c * FF_CW, D_FF + c * FF_CW)]
        ys = up(0)
        for c in range(n_chunks):
            nxt = up(c + 1) if c + 1 < n_chunks else None
            ug = conv(ys[0], c * FF_CW)
            uv = conv(ys[1], D_FF + c * FF_CW)
            f_ref[t, :, c * FF_CW:(c + 1) * FF_CW] = (ug * jax.nn.sigmoid(ug) * uv).astype(BF16)
            ys = nxt
        return jnp.dot(f_ref[t], wdn_ref[...], preferred_element_type=F32)

    def post(t, acc):
        rows = slice(t * ts, (t + 1) * ts)
        o_ref[0, rows, :] = (o_ref[0, rows, :]
                             + gt2 * (permute_rows(_rms(acc), 2 * t + 1) * gpo_ref[...]))

    hs = [pre(0)]
    for t in range(n_sub):
        if t + 1 < n_sub:
            hs.append(pre(t + 1))
        acc = mlp(t, hs[t])
        post(t, acc)


def _ffn_call(o_da, o_ret, x, mod, g_post_mix, g_pre_ffn, g_post_ffn, w_out_b, w_up_b,
              conv_w, conv_b, w_down_b):
    b, s, d = x.shape
    tm = TM_FFN
    const = dict(pipeline_mode=pl.Buffered(1))
    vec = lambda n: pl.BlockSpec((1, n), lambda bi, i: (0, 0))
    return pl.pallas_call(
        _ffn_kernel,
        grid=(b, s // tm),
        in_specs=[pl.BlockSpec((1, tm, DA_WIDTH), lambda bi, i: (bi, i, 0)),
                  pl.BlockSpec((1, tm, RET_WIDTH), lambda bi, i: (bi, i, 0)),
                  pl.BlockSpec((1, tm, d), lambda bi, i: (bi, i, 0)),
                  pl.BlockSpec((1, N_MOD, d), lambda bi, i: (bi, 0, 0)),
                  vec(d), vec(d), vec(d),
                  pl.BlockSpec((d, d), lambda bi, i: (0, 0), **const),
                  pl.BlockSpec((d, 2 * D_FF), lambda bi, i: (0, 0), **const),
                  pl.BlockSpec((3, 2 * D_FF), lambda bi, i: (0, 0)),
                  pl.BlockSpec((1, 2 * D_FF), lambda bi, i: (0, 0)),
                  pl.BlockSpec((D_FF, d), lambda bi, i: (0, 0), **const)],
        out_specs=pl.BlockSpec((1, tm, d), lambda bi, i: (bi, i, 0)),
        out_shape=jax.ShapeDtypeStruct((b, s, d), F32),
        scratch_shapes=[pltpu.VMEM((2 * (tm // FFN_SUB), d // 128, FFN_SUB, 128), F32),
                        pltpu.VMEM((tm // FFN_SUB, FFN_SUB, D_FF), BF16),
                        pltpu.VMEM((2, 8, 2 * D_FF), F32)],
        compiler_params=pltpu.CompilerParams(dimension_semantics=("arbitrary", "arbitrary"),
                                             vmem_limit_bytes=VMEM_LIMIT_BYTES),
        name="ffn",
    )(o_da, o_ret, x, mod, g_post_mix, g_pre_ffn, g_post_ffn, w_out_b, w_up_b,
      conv_w, conv_b, w_down_b)


def kernel(x, c, w_ada, b_ada, g_pre_mix, w_in, lam_q1, lam_k1, lam_q2, lam_k2, g_da_subln,
           w_out, g_post_mix, g_pre_ffn, w_up, conv_w, conv_b, w_down, g_post_ffn):
    b, s, d = x.shape
    depth = w_ada.shape[0]
    assert depth == 1 and d == D_MODEL and s % TM_IN == 0 and s % TQ == 0
    for l in range(depth):
        mod = _mod_call(c, w_ada[l], b_ada[l]).reshape(b, N_MOD, d)
        w_in_b = w_in[l].astype(BF16)
        w_vt_b = w_in[l][:, 1024:1536].T.astype(BF16)
        qk, vt, ret = _inproj_call(x, mod, g_pre_mix[l][None], w_in_b, w_vt_b)
        o_da = _dattn_call(qk, vt, lam_q1[l][None], lam_k1[l][None], lam_q2[l][None],
                           lam_k2[l][None], g_da_subln[l][None])
        o_ret = _ret_call(ret)
        x = _ffn_call(o_da, o_ret, x, mod, g_post_mix[l][None], g_pre_ffn[l][None],
                      g_post_ffn[l][None], w_out[l].astype(BF16), w_up[l].astype(BF16),
                      conv_w[l], conv_b[l][None], w_down[l].astype(BF16))
    return x
```

```python
import functools
import math

import jax
import jax.numpy as jnp
from jax import lax
from jax.experimental import pallas as pl
from jax.experimental.pallas import tpu as pltpu

D_MODEL = 1024
CHUNK = 64
DA_HEADS = 4
DA_QK = 64
DA_V = 128
DA_WIDTH = 512
RET_HEADS = 4
RET_QK = 64
RET_V = 128
RET_WIDTH = 512
IN_WIDTH = 3072
D_FF = 2816
N_MOD = 6
RMS_EPS = 1e-6
LAMBDA_INIT = 0.8 - 0.6 * math.exp(-0.3 * 0)

LOG2E = 1.4426950408889634
Q_FOLD = (DA_QK ** -0.5) * LOG2E
RET_K_FOLD = RET_QK ** -0.5

VMEM_LIMIT_BYTES = 56 * 1024 * 1024

BF16 = jnp.bfloat16
F32 = jnp.float32

TM_IN = 512
TQ = 512
TK = 512
RET_L = 256
TM_FFN = 512
FFN_SUB = 256
FF_CW = 256


def _rms(x):
    return x * lax.rsqrt(jnp.mean(x * x, axis=-1, keepdims=True) + RMS_EPS)


def _mod_kernel(c_ref, w_ref, b_ref, o_ref):
    c = c_ref[...]
    a = c * jax.nn.sigmoid(c)
    o_ref[...] = jnp.dot(a, w_ref[...], preferred_element_type=F32,
                         precision=lax.Precision.HIGHEST) + b_ref[...]


def _mod_call(c, w_ada, b_ada):
    b, d = c.shape
    n = w_ada.shape[1]
    tn = 1024
    return pl.pallas_call(
        _mod_kernel,
        grid=(n // tn,),
        in_specs=[pl.BlockSpec((b, d), lambda j: (0, 0)),
                  pl.BlockSpec((d, tn), lambda j: (0, j)),
                  pl.BlockSpec((1, tn), lambda j: (0, j))],
        out_specs=pl.BlockSpec((b, tn), lambda j: (0, j)),
        out_shape=jax.ShapeDtypeStruct((b, n), F32),
        compiler_params=pltpu.CompilerParams(dimension_semantics=("arbitrary",),
                                             vmem_limit_bytes=VMEM_LIMIT_BYTES),
        name="mod",
    )(c, w_ada, b_ada.reshape(1, n))


def _inproj_kernel(x_ref, mod_ref, g_ref, w_ref, wvt_ref, qk_ref, vt_ref, ret_ref):
    x = x_ref[0]
    sh = mod_ref[0, 0:1, :]
    sc = mod_ref[0, 1:2, :]
    h = (_rms(x) * g_ref[...]) * (1.0 + sc) + sh
    hb = h.astype(BF16)

    def proj(lo, hi):
        return jnp.dot(hb, w_ref[:, lo:hi], preferred_element_type=F32)

    qk_ref[0, :, 0:512] = (proj(0, 512) * Q_FOLD).astype(BF16)
    qk_ref[0, :, 512:1024] = proj(512, 1024).astype(BF16)
    vt = lax.dot_general(wvt_ref[...], hb, (((1,), (1,)), ((), ())),
                         preferred_element_type=F32).astype(BF16)
    for j in range(vt_ref.shape[1]):
        vt_ref[0, j] = vt[:, j * TK:(j + 1) * TK]
    ret_ref[0, :, 0:256] = proj(1536, 1792).astype(BF16)
    ret_ref[0, :, 256:512] = (proj(1792, 2048) * RET_K_FOLD).astype(BF16)
    ret_ref[0, :, 512:1024] = proj(2048, 2560).astype(BF16)
    ret_ref[0, :, 1024:1536] = proj(2560, 3072).astype(BF16)


def _inproj_call(x, mod, g_pre_mix, w_in_b, w_vt_b):
    b, s, d = x.shape
    tm = TM_IN
    const = dict(pipeline_mode=pl.Buffered(1))
    return pl.pallas_call(
        _inproj_kernel,
        grid=(b, s // tm),
        in_specs=[pl.BlockSpec((1, tm, d), lambda bi, i: (bi, i, 0)),
                  pl.BlockSpec((1, N_MOD, d), lambda bi, i: (bi, 0, 0)),
                  pl.BlockSpec((1, d), lambda bi, i: (0, 0)),
                  pl.BlockSpec((d, IN_WIDTH), lambda bi, i: (0, 0), **const),
                  pl.BlockSpec((DA_WIDTH, d), lambda bi, i: (0, 0), **const)],
        out_specs=[pl.BlockSpec((1, tm, 1024), lambda bi, i: (bi, i, 0)),
                   pl.BlockSpec((1, tm // TK, DA_WIDTH, TK), lambda bi, i: (bi, i, 0, 0)),
                   pl.BlockSpec((1, tm, 1536), lambda bi, i: (bi, i, 0))],
        out_shape=[jax.ShapeDtypeStruct((b, s, 1024), BF16),
                   jax.ShapeDtypeStruct((b, s // TK, DA_WIDTH, TK), BF16),
                   jax.ShapeDtypeStruct((b, s, 1536), BF16)],
        compiler_params=pltpu.CompilerParams(dimension_semantics=("arbitrary", "arbitrary"),
                                             vmem_limit_bytes=VMEM_LIMIT_BYTES),
        name="inproj",
    )(x, mod, g_pre_mix, w_in_b, w_vt_b)


def _dattn_kernel(q_ref, k_ref, vt_ref, lq1_ref, lk1_ref, lq2_ref, lk2_ref, gsub_ref,
                  o_ref, dbias_ref, cpos_ref, acc_ref, *bufs):
    tbufs, pbufs = bufs[:DA_HEADS], bufs[DA_HEADS:]
    qi = pl.program_id(1)
    slopes = [LOG2E * 2.0 ** (-8.0 * (h + 1) / DA_HEADS) for h in range(DA_HEADS)]

    lane = lax.broadcasted_iota(jnp.int32, (TQ, 128), 1)

    @pl.when(qi == 0)
    def _():
        r = lax.broadcasted_iota(jnp.int32, (TK, 128), 0).astype(F32)
        kl = lax.broadcasted_iota(jnp.int32, (TK, 128), 1)
        c = lax.broadcasted_iota(jnp.int32, (TK, TQ), 0)
        a = lax.broadcasted_iota(jnp.int32, (TK, TQ), 1)
        rel = (a - jnp.abs(a - c)).astype(F32)
        allowed = (c // CHUNK) <= (a // CHUNK)
        for h in range(DA_HEADS):
            pbufs[h][...] = jnp.zeros((TK, 2 * TQ), BF16)
            dbias_ref[h] = jnp.where(allowed, slopes[h] * rel, -jnp.inf)
            x = slopes[h] * r
            hi = x.astype(BF16).astype(F32)
            r1 = x - hi
            mid = r1.astype(BF16).astype(F32)
            lo = r1 - mid
            cpos_ref[h] = jnp.where(kl == 0, hi, jnp.where(kl == 1, mid,
                                    jnp.where(kl == 2, lo, 0.0))).astype(BF16)

    ones_lanes = jnp.where(lax.broadcasted_iota(jnp.int32, (2 * TQ, 128), 1) < 3, 1.0, 0.0)
    ones_rows = jnp.ones((16, TK), BF16)
    qqt = []
    for h in range(DA_HEADS):
        q = q_ref[0, :, 128 * h:128 * h + 128].astype(F32)
        qq = jnp.concatenate([jnp.where(lane < DA_QK, q, 0.0),
                              jnp.where(lane >= DA_QK, q, 0.0)], axis=0)
        qqt.append(jnp.concatenate([qq, ones_lanes], axis=1).T.astype(BF16))

    def kblock(h, kb):
        return k_ref[0, pl.ds(pl.multiple_of(kb * TK, TK), TK), 128 * h:128 * h + 128]

    def diag_scores(h):
        db = dbias_ref[h]
        s = jnp.dot(kblock(h, qi), qqt[h][0:128], preferred_element_type=F32)
        return s + jnp.concatenate([db, db], axis=1)

    def past_scores(h, kb):
        lhs = jnp.concatenate([kblock(h, kb), cpos_ref[h]], axis=1)
        return jnp.dot(lhs, qqt[h], preferred_element_type=F32)

    def score_stage(h, t):
        tbufs[h][...] = t
        return jnp.max(t, axis=0, keepdims=True)

    def softmax_stage(h, tmax, m, shift):
        m_new = jnp.maximum(m, tmax + shift)
        alpha = jnp.exp2(m - m_new)
        pbufs[h][...] = jnp.exp2(tbufs[h][...] - (m_new - shift)).astype(BF16)
        return alpha, m_new

    def value_stage(h, kb, alpha, l, pending=True):
        lhs = jnp.concatenate([vt_ref[0, kb, 128 * h:128 * h + 128, :], ones_rows], axis=0)
        lhs = jnp.where(pending, lhs, jnp.zeros_like(lhs))
        pv = jnp.dot(lhs, pbufs[h][...], preferred_element_type=F32)
        acc_ref[h] = alpha * acc_ref[h] + pv[0:DA_V]
        return alpha * l + pv[DA_V:DA_V + 1]

    heads = range(DA_HEADS)
    for h in heads:
        acc_ref[h] = jnp.zeros((DA_V, 2 * TQ), F32)
    ms = [jnp.full((1, 2 * TQ), -jnp.inf, F32) for _ in heads]
    ls = [jnp.zeros((1, 2 * TQ), F32) for _ in heads]
    alphas = [jnp.ones((1, 2 * TQ), F32) for _ in heads]
    tmaxs = [score_stage(h, diag_scores(h)) for h in heads]

    def step(n, carry):
        ms, ls, alphas, tmaxs = (list(carry[4 * j:4 * j + 4]) for j in range(4))
        kb_value = jnp.where(n == 1, qi, jnp.maximum(n - 2, 0))
        kb_score = jnp.minimum(n, jnp.maximum(qi - 1, 0))
        rel = jnp.where(n == 0, 0, (n - 1 - qi) * TK).astype(F32)
        for h in heads:
            ls[h] = value_stage(h, kb_value, alphas[h], ls[h], pending=n > 0)
        for h in heads:
            alphas[h], ms[h] = softmax_stage(h, tmaxs[h], ms[h], slopes[h] * rel)
            tmaxs[h] = score_stage(h, past_scores(h, kb_score))
        return tuple(ms) + tuple(ls) + tuple(alphas) + tuple(tmaxs)

    carry = lax.fori_loop(0, qi + 1, step, tuple(ms) + tuple(ls) + tuple(alphas) + tuple(tmaxs))
    ms, ls, alphas = (list(carry[4 * j:4 * j + 4]) for j in range(3))
    kb_last = jnp.maximum(qi - 1, 0)
    for h in heads:
        ls[h] = value_stage(h, kb_last, alphas[h], ls[h])
    carry = [x for h in heads for x in (ms[h], ls[h])]

    lam = (jnp.exp(jnp.sum(lq1_ref[...] * lk1_ref[...], axis=-1, keepdims=True))
           - jnp.exp(jnp.sum(lq2_ref[...] * lk2_ref[...], axis=-1, keepdims=True))
           + LAMBDA_INIT)
    for h in range(DA_HEADS):
        inv = 1.0 / carry[2 * h + 1]
        acc = acc_ref[h]
        ot = acc[:, :TQ] * inv[:, :TQ] - lam * (acc[:, TQ:] * inv[:, TQ:])
        ot = ot * lax.rsqrt(jnp.mean(ot * ot, axis=0, keepdims=True) + RMS_EPS)
        o = ot.T * gsub_ref[...] * (1.0 - LAMBDA_INIT)
        o_ref[0, :, 128 * h:128 * h + 128] = o.astype(BF16)


def _dattn_call(qk, vt, lam_q1, lam_k1, lam_q2, lam_k2, g_sub):
    b, s, _ = qk.shape
    nkv = s // TK
    vec = lambda n: pl.BlockSpec((1, n), lambda bi, i: (0, 0))
    return pl.pallas_call(
        _dattn_kernel,
        grid=(b, s // TQ),
        in_specs=[pl.BlockSpec((1, TQ, DA_WIDTH), lambda bi, i: (bi, i, 0)),
                  pl.BlockSpec((1, s, DA_WIDTH), lambda bi, i: (bi, 0, 1)),
                  pl.BlockSpec((1, nkv, DA_WIDTH, TK), lambda bi, i: (bi, 0, 0, 0)),
                  vec(DA_QK), vec(DA_QK), vec(DA_QK), vec(DA_QK), vec(DA_V)],
        out_specs=pl.BlockSpec((1, TQ, DA_WIDTH), lambda bi, i: (bi, i, 0)),
        out_shape=jax.ShapeDtypeStruct((b, s, DA_WIDTH), BF16),
        scratch_shapes=[pltpu.VMEM((DA_HEADS, TK, TQ), F32),
                        pltpu.VMEM((DA_HEADS, TK, 128), BF16),
                        pltpu.VMEM((DA_HEADS, DA_V, 2 * TQ), F32)]
                       + [pltpu.VMEM((TK, 2 * TQ), F32) for _ in range(DA_HEADS)]
                       + [pltpu.VMEM((TK, 2 * TQ), BF16) for _ in range(DA_HEADS)],
        compiler_params=pltpu.CompilerParams(
            dimension_semantics=("arbitrary", "arbitrary"),
            vmem_limit_bytes=VMEM_LIMIT_BYTES),
        name="dattn",
    )(qk, qk, vt, lam_q1, lam_k1, lam_q2, lam_k2, g_sub)


def _ret_kernel(q_ref, k_ref, v_ref, g_ref, o_ref, dec_ref, state_ref):
    i = pl.program_id(1)
    L = RET_L
    row = lax.broadcasted_iota(jnp.int32, (L, L), 0)
    col = lax.broadcasted_iota(jnp.int32, (L, L), 1)

    @pl.when(i == 0)
    def _():
        state_ref[...] = jnp.zeros_like(state_ref)
        d = row - col
        same = (row // CHUNK) == (col // CHUNK)
        past = (col // CHUNK) < (row // CHUNK)
        e = jnp.where(same, jnp.abs(d), d).astype(F32)
        for h in range(RET_HEADS):
            lg = math.log(1.0 - 2.0 ** (-5.0 - h))
            dec_ref[h] = jnp.where(same | past, jnp.exp(lg * e), 0.0)

    lane = lax.broadcasted_iota(jnp.int32, (L, 128), 1)
    r1 = lax.broadcasted_iota(jnp.int32, (L, 1), 0).astype(F32)
    for h in range(RET_HEADS):
        lg = math.log(1.0 - 2.0 ** (-5.0 - h))
        pair = slice(128 * (h // 2), 128 * (h // 2) + 128)
        lo = RET_QK * (h % 2)
        sel = (lane >= lo) & (lane < lo + RET_QK)
        qp = q_ref[0, :, pair]
        kp = k_ref[0, :, pair]
        qz = jnp.where(sel, qp, jnp.zeros_like(qp))
        kz = jnp.where(sel, kp, jnp.zeros_like(kp))
        vh = v_ref[0, :, 128 * h:128 * h + 128]
        s = lax.dot_general(qz, kz, (((1,), (1,)), ((), ())), preferred_element_type=F32)
        s = s * dec_ref[h]
        o = jnp.dot(s.astype(BF16), vh, preferred_element_type=F32)
        st = state_ref[h]
        out_dec = jnp.exp(lg * (r1 + 1.0))
        o = o + out_dec * jnp.dot(qz, st.astype(BF16), preferred_element_type=F32)
        in_dec = jnp.exp(lg * (L - 1.0 - r1))
        kd = (kz.astype(F32) * in_dec).astype(BF16)
        u = lax.dot_general(kd, vh, (((0,), (0,)), ((), ())), preferred_element_type=F32)
        state_ref[h] = math.exp(lg * L) * st + u
        g = g_ref[0, :, 128 * h:128 * h + 128].astype(F32)
        o = _rms(o) * (g * jax.nn.sigmoid(g))
        o_ref[0, :, 128 * h:128 * h + 128] = o.astype(BF16)


def _ret_call(ret):
    b, s, _ = ret.shape
    L = RET_L
    return pl.pallas_call(
        _ret_kernel,
        grid=(b, s // L),
        in_specs=[pl.BlockSpec((1, L, 256), lambda bi, i: (bi, i, 0)),
                  pl.BlockSpec((1, L, 256), lambda bi, i: (bi, i, 1)),
                  pl.BlockSpec((1, L, 512), lambda bi, i: (bi, i, 1)),
                  pl.BlockSpec((1, L, 512), lambda bi, i: (bi, i, 2))],
        out_specs=pl.BlockSpec((1, L, RET_WIDTH), lambda bi, i: (bi, i, 0)),
        out_shape=jax.ShapeDtypeStruct((b, s, RET_WIDTH), BF16),
        scratch_shapes=[pltpu.VMEM((RET_HEADS, L, L), F32),
                        pltpu.VMEM((RET_HEADS, 128, RET_V), F32)],
        compiler_params=pltpu.CompilerParams(dimension_semantics=("arbitrary", "arbitrary"),
                                             vmem_limit_bytes=VMEM_LIMIT_BYTES),
        name="ret",
    )(ret, ret, ret, ret)


def _ffn_kernel(oda_ref, oret_ref, x_ref, mod_ref, gpm_ref, gpf_ref, gpo_ref,
                wout_ref, wup_ref, cw_ref, cb_ref, wdn_ref, o_ref, perm_ref, f_ref, carry_ref):
    i = pl.program_id(1)
    ts = FFN_SUB
    n_sub = x_ref.shape[1] // ts
    n_groups = ts // 64

    @pl.when(i == 0)
    def _():
        carry_ref[...] = jnp.zeros_like(carry_ref)

    def permute_rows(a, slot):
        n = a.shape[1] // 128
        for s in range(n):
            perm_ref[slot, s] = a[:, 128 * s:128 * s + 128]
        rows = []
        for g in range(n_groups):
            for r in range(8):
                rows.append(jnp.concatenate(
                    [perm_ref[slot, s, pl.ds(64 * g + r, 8, stride=8), :] for s in range(n)],
                    axis=1))
        return jnp.concatenate(rows, axis=0)

    gt1 = mod_ref[0, 2:3, :]
    sh2 = mod_ref[0, 3:4, :]
    sc2 = mod_ref[0, 4:5, :]
    gt2 = mod_ref[0, 5:6, :]
    sub0 = lax.broadcasted_iota(jnp.int32, (8, FF_CW), 0) == 0
    n_chunks = D_FF // FF_CW

    def pre(t):
        rows = slice(t * ts, (t + 1) * ts)
        mix = (jnp.dot(oda_ref[0, rows, :], wout_ref[0:DA_WIDTH, :], preferred_element_type=F32)
               + jnp.dot(oret_ref[0, rows, :], wout_ref[DA_WIDTH:, :],
                         preferred_element_type=F32))
        x1 = x_ref[0, rows, :] + gt1 * (_rms(mix) * gpm_ref[...])
        o_ref[0, rows, :] = x1
        return permute_rows((_rms(x1) * gpf_ref[...]) * (1.0 + sc2) + sh2, t % 2).astype(BF16)

    def conv(y, c0):
        cols = slice(c0, c0 + FF_CW)
        w0, w1, w2, cb = cw_ref[0:1, cols], cw_ref[1:2, cols], cw_ref[2:3, cols], cb_ref[:, cols]
        prev7 = carry_ref[0, :, cols]
        prev6 = carry_ref[1, :, cols]
        out = []
        for g in range(n_groups):
            yg = y[64 * g:64 * g + 64]
            rot7 = pltpu.roll(yg[56:64], 1, 0)
            rot6 = pltpu.roll(yg[48:56], 1, 0)
            sp1 = jnp.where(sub0, prev7, rot7)
            sp2 = jnp.where(sub0, prev6, rot6)
            y1 = jnp.concatenate([sp1, yg[0:56]], axis=0)
            y2 = jnp.concatenate([sp2, sp1, yg[0:48]], axis=0)
            out.append(cb + w0 * y2 + w1 * y1 + w2 * yg)
            prev7, prev6 = rot7, rot6
        carry_ref[0, :, cols] = prev7
        carry_ref[1, :, cols] = prev6
        return jnp.concatenate(out, axis=0)

    def mlp(t, h):
        def up(c):
            return [jnp.dot(h, wup_ref[:, c0:c0 + FF_CW], preferred_element_type=F32)
                    for c0 in (c * FF_CW, D_FF + c * FF_CW)]
        ys = up(0)
        for c in range(n_chunks):
            nxt = up(c + 1) if c + 1 < n_chunks else None
            ug = conv(ys[0], c * FF_CW)
            uv = conv(ys[1], D_FF + c * FF_CW)
            f_ref[t, :, c * FF_CW:(c + 1) * FF_CW] = (ug * jax.nn.sigmoid(ug) * uv).astype(BF16)
            ys = nxt
        return jnp.dot(f_ref[t], wdn_ref[...], preferred_element_type=F32)

    def post(t, acc):
        rows = slice(t * ts, (t + 1) * ts)
        o_ref[0, rows, :] = (o_ref[0, rows, :]
                             + gt2 * (permute_rows(_rms(acc), 2 + t % 2) * gpo_ref[...]))

    hs = [pre(0)]
    for t in range(n_sub):
        if t + 1 < n_sub:
            hs.append(pre(t + 1))
        acc = mlp(t, hs[t])
        post(t, acc)


def _ffn_call(o_da, o_ret, x, mod, g_post_mix, g_pre_ffn, g_post_ffn, w_out_b, w_up_b,
              conv_w, conv_b, w_down_b):
    b, s, d = x.shape
    tm = TM_FFN
    const = dict(pipeline_mode=pl.Buffered(1))
    vec = lambda n: pl.BlockSpec((1, n), lambda bi, i: (0, 0))
    return pl.pallas_call(
        _ffn_kernel,
        grid=(b, s // tm),
        in_specs=[pl.BlockSpec((1, tm, DA_WIDTH), lambda bi, i: (bi, i, 0)),
                  pl.BlockSpec((1, tm, RET_WIDTH), lambda bi, i: (bi, i, 0)),
                  pl.BlockSpec((1, tm, d), lambda bi, i: (bi, i, 0)),
                  pl.BlockSpec((1, N_MOD, d), lambda bi, i: (bi, 0, 0)),
                  vec(d), vec(d), vec(d),
                  pl.BlockSpec((d, d), lambda bi, i: (0, 0), **const),
                  pl.BlockSpec((d, 2 * D_FF), lambda bi, i: (0, 0), **const),
                  pl.BlockSpec((3, 2 * D_FF), lambda bi, i: (0, 0)),
                  pl.BlockSpec((1, 2 * D_FF), lambda bi, i: (0, 0)),
                  pl.BlockSpec((D_FF, d), lambda bi, i: (0, 0), **const)],
        out_specs=pl.BlockSpec((1, tm, d), lambda bi, i: (bi, i, 0)),
        out_shape=jax.ShapeDtypeStruct((b, s, d), F32),
        scratch_shapes=[pltpu.VMEM((4, d // 128, FFN_SUB, 128), F32),
                        pltpu.VMEM((tm // FFN_SUB, FFN_SUB, D_FF), BF16),
                        pltpu.VMEM((2, 8, 2 * D_FF), F32)],
        compiler_params=pltpu.CompilerParams(dimension_semantics=("arbitrary", "arbitrary"),
                                             vmem_limit_bytes=VMEM_LIMIT_BYTES),
        name="ffn",
    )(o_da, o_ret, x, mod, g_post_mix, g_pre_ffn, g_post_ffn, w_out_b, w_up_b,
      conv_w, conv_b, w_down_b)


def kernel(x, c, w_ada, b_ada, g_pre_mix, w_in, lam_q1, lam_k1, lam_q2, lam_k2, g_da_subln,
           w_out, g_post_mix, g_pre_ffn, w_up, conv_w, conv_b, w_down, g_post_ffn):
    b, s, d = x.shape
    depth = w_ada.shape[0]
    assert depth == 1 and d == D_MODEL and s % TM_IN == 0 and s % TQ == 0
    for l in range(depth):
        mod = _mod_call(c, w_ada[l], b_ada[l]).reshape(b, N_MOD, d)
        w_in_b = w_in[l].astype(BF16)
        w_vt_b = w_in[l][:, 1024:1536].T.astype(BF16)
        qk, vt, ret = _inproj_call(x, mod, g_pre_mix[l][None], w_in_b, w_vt_b)
        o_da = _dattn_call(qk, vt, lam_q1[l][None], lam_k1[l][None], lam_q2[l][None],
                           lam_k2[l][None], g_da_subln[l][None])
        o_ret = _ret_call(ret)
        x = _ffn_call(o_da, o_ret, x, mod, g_post_mix[l][None], g_pre_ffn[l][None],
                      g_post_ffn[l][None], w_out[l].astype(BF16), w_up[l].astype(BF16),
                      conv_w[l], conv_b[l][None], w_down[l].astype(BF16))
    return x
```

```python
import functools
import math

import jax
import jax.numpy as jnp
from jax import lax
from jax.experimental import pallas as pl
from jax.experimental.pallas import tpu as pltpu

D_MODEL = 1024
CHUNK = 64
DA_HEADS = 4
DA_QK = 64
DA_V = 128
DA_WIDTH = 512
RET_HEADS = 4
RET_QK = 64
RET_V = 128
RET_WIDTH = 512
IN_WIDTH = 3072
D_FF = 2816
N_MOD = 6
RMS_EPS = 1e-6
LAMBDA_INIT = 0.8 - 0.6 * math.exp(-0.3 * 0)

LOG2E = 1.4426950408889634
Q_FOLD = (DA_QK ** -0.5) * LOG2E
RET_K_FOLD = RET_QK ** -0.5

VMEM_LIMIT_BYTES = 56 * 1024 * 1024

BF16 = jnp.bfloat16
F32 = jnp.float32

TM_IN = 512
TQ = 512
TK = 512
RET_L = 256
TM_FFN = 512
FFN_SUB = 256
FF_CW = 256


def _rms(x):
    return x * lax.rsqrt(jnp.mean(x * x, axis=-1, keepdims=True) + RMS_EPS)


def _mod_kernel(c_ref, w_ref, b_ref, o_ref):
    c = c_ref[...]
    a = c * jax.nn.sigmoid(c)
    o_ref[...] = jnp.dot(a, w_ref[...], preferred_element_type=F32,
                         precision=lax.Precision.HIGHEST) + b_ref[...]


def _mod_call(c, w_ada, b_ada):
    b, d = c.shape
    n = w_ada.shape[1]
    tn = 1024
    return pl.pallas_call(
        _mod_kernel,
        grid=(n // tn,),
        in_specs=[pl.BlockSpec((b, d), lambda j: (0, 0)),
                  pl.BlockSpec((d, tn), lambda j: (0, j)),
                  pl.BlockSpec((1, tn), lambda j: (0, j))],
        out_specs=pl.BlockSpec((b, tn), lambda j: (0, j)),
        out_shape=jax.ShapeDtypeStruct((b, n), F32),
        compiler_params=pltpu.CompilerParams(dimension_semantics=("arbitrary",),
                                             vmem_limit_bytes=VMEM_LIMIT_BYTES),
        name="mod",
    )(c, w_ada, b_ada.reshape(1, n))


def _inproj_kernel(x_ref, mod_ref, g_ref, w_ref, wvt_ref, qk_ref, vt_ref, oret_ref,
                   dec_ref, rowdec_ref, state_ref):
    i = pl.program_id(1)
    tm = x_ref.shape[1]
    L = RET_L
    lgs = [math.log(1.0 - 2.0 ** (-5.0 - h)) for h in range(RET_HEADS)]

    @pl.when(i == 0)
    def _():
        state_ref[...] = jnp.zeros_like(state_ref)
        row = lax.broadcasted_iota(jnp.int32, (L, L), 0)
        col = lax.broadcasted_iota(jnp.int32, (L, L), 1)
        d = row - col
        same = (row // CHUNK) == (col // CHUNK)
        past = (col // CHUNK) < (row // CHUNK)
        e = jnp.where(same, jnp.abs(d), d).astype(F32)
        r = lax.broadcasted_iota(jnp.int32, (L, 128), 0).astype(F32)
        for h in range(RET_HEADS):
            dec_ref[h] = jnp.where(same | past, jnp.exp(lgs[h] * e), 0.0)
            rowdec_ref[0, h] = jnp.exp(lgs[h] * (r + 1.0))
            rowdec_ref[1, h] = jnp.exp(lgs[h] * (L - 1.0 - r))

    x = x_ref[0]
    sh = mod_ref[0, 0:1, :]
    sc = mod_ref[0, 1:2, :]
    hb = ((_rms(x) * g_ref[...]) * (1.0 + sc) + sh).astype(BF16)

    def proj(lo, hi):
        return jnp.dot(hb, w_ref[:, lo:hi], preferred_element_type=F32)

    rq = proj(1536, 1792).astype(BF16)
    rk = (proj(1792, 2048) * RET_K_FOLD).astype(BF16)
    rv = proj(2048, 2560).astype(BF16)
    rg = proj(2560, 3072)

    lane = lax.broadcasted_iota(jnp.int32, (L, 128), 1)
    nt = (((1,), (1,)), ((), ()))
    blocks = [(t, h) for t in range(tm // L) for h in range(RET_HEADS)]
    qz, kz, vh, sb, kd = {}, {}, {}, {}, {}
    for t, h in blocks:
        rows = slice(t * L, (t + 1) * L)
        pair = slice(128 * (h // 2), 128 * (h // 2) + 128)
        lo = RET_QK * (h % 2)
        sel = (lane >= lo) & (lane < lo + RET_QK)
        qp, kp = rq[rows, pair], rk[rows, pair]
        qz[t, h] = jnp.where(sel, qp, jnp.zeros_like(qp))
        kz[t, h] = jnp.where(sel, kp, jnp.zeros_like(kp))
        vh[t, h] = rv[rows, 128 * h:128 * h + 128]
        s = lax.dot_general(qz[t, h], kz[t, h], nt, preferred_element_type=F32)
        sb[t, h] = (s * dec_ref[h]).astype(BF16)
        kd[t, h] = (kz[t, h].astype(F32) * rowdec_ref[1, h]).astype(BF16)

    qk_ref[0, :, 0:512] = (proj(0, 512) * Q_FOLD).astype(BF16)

    def retention_block(t):
        rows = slice(t * L, (t + 1) * L)
        for h in range(RET_HEADS):
            st = state_ref[h]
            o = jnp.dot(sb[t, h], vh[t, h], preferred_element_type=F32)
            o = o + rowdec_ref[0, h] * jnp.dot(qz[t, h], st.astype(BF16),
                                               preferred_element_type=F32)
            u = lax.dot_general(kd[t, h], vh[t, h], (((0,), (0,)), ((), ())),
                                preferred_element_type=F32)
            state_ref[h] = math.exp(lgs[h] * L) * st + u
            g = rg[rows, 128 * h:128 * h + 128]
            oret_ref[0, rows, 128 * h:128 * h + 128] = (
                _rms(o) * (g * jax.nn.sigmoid(g))).astype(BF16)

    retention_block(0)
    qk_ref[0, :, 512:1024] = proj(512, 1024).astype(BF16)
    for t in range(1, tm // L):
        retention_block(t)
    vt = lax.dot_general(wvt_ref[...], hb, nt, preferred_element_type=F32).astype(BF16)
    for j in range(vt_ref.shape[1]):
        vt_ref[0, j] = vt[:, j * TK:(j + 1) * TK]


def _inproj_call(x, mod, g_pre_mix, w_in_b, w_vt_b):
    b, s, d = x.shape
    tm = TM_IN
    const = dict(pipeline_mode=pl.Buffered(1))
    return pl.pallas_call(
        _inproj_kernel,
        grid=(b, s // tm),
        in_specs=[pl.BlockSpec((1, tm, d), lambda bi, i: (bi, i, 0)),
                  pl.BlockSpec((1, N_MOD, d), lambda bi, i: (bi, 0, 0)),
                  pl.BlockSpec((1, d), lambda bi, i: (0, 0)),
                  pl.BlockSpec((d, IN_WIDTH), lambda bi, i: (0, 0), **const),
                  pl.BlockSpec((DA_WIDTH, d), lambda bi, i: (0, 0), **const)],
        out_specs=[pl.BlockSpec((1, tm, 1024), lambda bi, i: (bi, i, 0)),
                   pl.BlockSpec((1, tm // TK, DA_WIDTH, TK), lambda bi, i: (bi, i, 0, 0)),
                   pl.BlockSpec((1, tm, RET_WIDTH), lambda bi, i: (bi, i, 0))],
        out_shape=[jax.ShapeDtypeStruct((b, s, 1024), BF16),
                   jax.ShapeDtypeStruct((b, s // TK, DA_WIDTH, TK), BF16),
                   jax.ShapeDtypeStruct((b, s, RET_WIDTH), BF16)],
        scratch_shapes=[pltpu.VMEM((RET_HEADS, RET_L, RET_L), F32),
                        pltpu.VMEM((2, RET_HEADS, RET_L, 128), F32),
                        pltpu.VMEM((RET_HEADS, 128, RET_V), F32)],
        compiler_params=pltpu.CompilerParams(dimension_semantics=("arbitrary", "arbitrary"),
                                             vmem_limit_bytes=VMEM_LIMIT_BYTES),
        name="inproj",
    )(x, mod, g_pre_mix, w_in_b, w_vt_b)


def _dattn_kernel(q_ref, k_ref, vt_ref, lq1_ref, lk1_ref, lq2_ref, lk2_ref, gsub_ref,
                  o_ref, dbias_ref, cpos_ref, acc_ref, *bufs):
    tbufs, pbufs = bufs[:DA_HEADS], bufs[DA_HEADS:]
    qi = pl.program_id(1)
    slopes = [LOG2E * 2.0 ** (-8.0 * (h + 1) / DA_HEADS) for h in range(DA_HEADS)]

    lane = lax.broadcasted_iota(jnp.int32, (TQ, 128), 1)

    @pl.when(qi == 0)
    def _():
        r = lax.broadcasted_iota(jnp.int32, (TK, 128), 0).astype(F32)
        kl = lax.broadcasted_iota(jnp.int32, (TK, 128), 1)
        c = lax.broadcasted_iota(jnp.int32, (TK, TQ), 0)
        a = lax.broadcasted_iota(jnp.int32, (TK, TQ), 1)
        rel = (a - jnp.abs(a - c)).astype(F32)
        allowed = (c // CHUNK) <= (a // CHUNK)
        for h in range(DA_HEADS):
            pbufs[h][...] = jnp.zeros((TK, 2 * TQ), BF16)
            dbias_ref[h] = jnp.where(allowed, slopes[h] * rel, -jnp.inf)
            x = slopes[h] * r
            hi = x.astype(BF16).astype(F32)
            r1 = x - hi
            mid = r1.astype(BF16).astype(F32)
            lo = r1 - mid
            cpos_ref[h] = jnp.where(kl == 0, hi, jnp.where(kl == 1, mid,
                                    jnp.where(kl == 2, lo, 0.0))).astype(BF16)

    ones_lanes = jnp.where(lax.broadcasted_iota(jnp.int32, (2 * TQ, 128), 1) < 3, 1.0, 0.0)
    ones_rows = jnp.ones((16, TK), BF16)
    qqt = []
    for h in range(DA_HEADS):
        q = q_ref[0, :, 128 * h:128 * h + 128].astype(F32)
        qq = jnp.concatenate([jnp.where(lane < DA_QK, q, 0.0),
                              jnp.where(lane >= DA_QK, q, 0.0)], axis=0)
        qqt.append(jnp.concatenate([qq, ones_lanes], axis=1).T.astype(BF16))

    def kblock(h, kb):
        return k_ref[0, pl.ds(pl.multiple_of(kb * TK, TK), TK), 128 * h:128 * h + 128]

    def diag_scores(h):
        db = dbias_ref[h]
        s = jnp.dot(kblock(h, qi), qqt[h][0:128], preferred_element_type=F32)
        return s + jnp.concatenate([db, db], axis=1)

    def past_scores(h, kb):
        lhs = jnp.concatenate([kblock(h, kb), cpos_ref[h]], axis=1)
        return jnp.dot(lhs, qqt[h], preferred_element_type=F32)

    def score_stage(h, t):
        tbufs[h][...] = t
        return jnp.max(t, axis=0, keepdims=True)

    def softmax_stage(h, tmax, m, shift):
        m_new = jnp.maximum(m, tmax + shift)
        alpha = jnp.exp2(m - m_new)
        pbufs[h][...] = jnp.exp2(tbufs[h][...] - (m_new - shift)).astype(BF16)
        return alpha, m_new

    def value_stage(h, kb, alpha, l, pending=True):
        lhs = jnp.concatenate([vt_ref[0, kb, 128 * h:128 * h + 128, :], ones_rows], axis=0)
        lhs = jnp.where(pending, lhs, jnp.zeros_like(lhs))
        pv = jnp.dot(lhs, pbufs[h][...], preferred_element_type=F32)
        acc_ref[h] = alpha * acc_ref[h] + pv[0:DA_V]
        return alpha * l + pv[DA_V:DA_V + 1]

    heads = range(DA_HEADS)
    for h in heads:
        acc_ref[h] = jnp.zeros((DA_V, 2 * TQ), F32)
    ms = [jnp.full((1, 2 * TQ), -jnp.inf, F32) for _ in heads]
    ls = [jnp.zeros((1, 2 * TQ), F32) for _ in heads]
    alphas = [jnp.ones((1, 2 * TQ), F32) for _ in heads]
    tmaxs = [score_stage(h, diag_scores(h)) for h in heads]

    def step(n, carry):
        ms, ls, alphas, tmaxs = (list(carry[4 * j:4 * j + 4]) for j in range(4))
        kb_value = jnp.where(n == 1, qi, jnp.maximum(n - 2, 0))
        kb_score = jnp.minimum(n, jnp.maximum(qi - 1, 0))
        rel = jnp.where(n == 0, 0, (n - 1 - qi) * TK).astype(F32)
        for h in heads:
            ls[h] = value_stage(h, kb_value, alphas[h], ls[h], pending=n > 0)
        for h in heads:
            alphas[h], ms[h] = softmax_stage(h, tmaxs[h], ms[h], slopes[h] * rel)
            tmaxs[h] = score_stage(h, past_scores(h, kb_score))
        return tuple(ms) + tuple(ls) + tuple(alphas) + tuple(tmaxs)

    carry = lax.fori_loop(0, qi + 1, step, tuple(ms) + tuple(ls) + tuple(alphas) + tuple(tmaxs))
    ms, ls, alphas = (list(carry[4 * j:4 * j + 4]) for j in range(3))
    kb_last = jnp.maximum(qi - 1, 0)
    for h in heads:
        ls[h] = value_stage(h, kb_last, alphas[h], ls[h])
    carry = [x for h in heads for x in (ms[h], ls[h])]

    lam = (jnp.exp(jnp.sum(lq1_ref[...] * lk1_ref[...], axis=-1, keepdims=True))
           - jnp.exp(jnp.sum(lq2_ref[...] * lk2_ref[...], axis=-1, keepdims=True))
           + LAMBDA_INIT)
    for h in range(DA_HEADS):
        inv = 1.0 / carry[2 * h + 1]
        acc = acc_ref[h]
        ot = acc[:, :TQ] * inv[:, :TQ] - lam * (acc[:, TQ:] * inv[:, TQ:])
        ot = ot * lax.rsqrt(jnp.mean(ot * ot, axis=0, keepdims=True) + RMS_EPS)
        o = ot.T * gsub_ref[...] * (1.0 - LAMBDA_INIT)
        o_ref[0, :, 128 * h:128 * h + 128] = o.astype(BF16)


def _dattn_call(qk, vt, lam_q1, lam_k1, lam_q2, lam_k2, g_sub):
    b, s, _ = qk.shape
    nkv = s // TK
    vec = lambda n: pl.BlockSpec((1, n), lambda bi, i: (0, 0))
    return pl.pallas_call(
        _dattn_kernel,
        grid=(b, s // TQ),
        in_specs=[pl.BlockSpec((1, TQ, DA_WIDTH), lambda bi, i: (bi, i, 0)),
                  pl.BlockSpec((1, s, DA_WIDTH), lambda bi, i: (bi, 0, 1)),
                  pl.BlockSpec((1, nkv, DA_WIDTH, TK), lambda bi, i: (bi, 0, 0, 0)),
                  vec(DA_QK), vec(DA_QK), vec(DA_QK), vec(DA_QK), vec(DA_V)],
        out_specs=pl.BlockSpec((1, TQ, DA_WIDTH), lambda bi, i: (bi, i, 0)),
        out_shape=jax.ShapeDtypeStruct((b, s, DA_WIDTH), BF16),
        scratch_shapes=[pltpu.VMEM((DA_HEADS, TK, TQ), F32),
                        pltpu.VMEM((DA_HEADS, TK, 128), BF16),
                        pltpu.VMEM((DA_HEADS, DA_V, 2 * TQ), F32)]
                       + [pltpu.VMEM((TK, 2 * TQ), F32) for _ in range(DA_HEADS)]
                       + [pltpu.VMEM((TK, 2 * TQ), BF16) for _ in range(DA_HEADS)],
        compiler_params=pltpu.CompilerParams(
            dimension_semantics=("arbitrary", "arbitrary"),
            vmem_limit_bytes=VMEM_LIMIT_BYTES),
        name="dattn",
    )(qk, qk, vt, lam_q1, lam_k1, lam_q2, lam_k2, g_sub)


def _ffn_kernel(oda_ref, oret_ref, x_ref, mod_ref, gpm_ref, gpf_ref, gpo_ref,
                wout_ref, wup_ref, cw_ref, cb_ref, wdn_ref, o_ref, perm_ref, f_ref, carry_ref):
    i = pl.program_id(1)
    ts = FFN_SUB
    n_sub = x_ref.shape[1] // ts
    n_groups = ts // 64

    @pl.when(i == 0)
    def _():
        carry_ref[...] = jnp.zeros_like(carry_ref)

    def permute_rows(a, slot):
        n = a.shape[1] // 128
        for s in range(n):
            perm_ref[slot, s] = a[:, 128 * s:128 * s + 128]
        rows = []
        for g in range(n_groups):
            for r in range(8):
                rows.append(jnp.concatenate(
                    [perm_ref[slot, s, pl.ds(64 * g + r, 8, stride=8), :] for s in range(n)],
                    axis=1))
        return jnp.concatenate(rows, axis=0)

    gt1 = mod_ref[0, 2:3, :]
    sh2 = mod_ref[0, 3:4, :]
    sc2 = mod_ref[0, 4:5, :]
    gt2 = mod_ref[0, 5:6, :]
    sub0 = lax.broadcasted_iota(jnp.int32, (8, FF_CW), 0) == 0
    n_chunks = D_FF // FF_CW

    def pre(t):
        rows = slice(t * ts, (t + 1) * ts)
        mix = (jnp.dot(oda_ref[0, rows, :], wout_ref[0:DA_WIDTH, :], preferred_element_type=F32)
               + jnp.dot(oret_ref[0, rows, :], wout_ref[DA_WIDTH:, :],
                         preferred_element_type=F32))
        x1 = x_ref[0, rows, :] + gt1 * (_rms(mix) * gpm_ref[...])
        o_ref[0, rows, :] = x1
        return permute_rows((_rms(x1) * gpf_ref[...]) * (1.0 + sc2) + sh2, t % 2).astype(BF16)

    def conv(y, c0):
        cols = slice(c0, c0 + FF_CW)
        w0, w1, w2, cb = cw_ref[0:1, cols], cw_ref[1:2, cols], cw_ref[2:3, cols], cb_ref[:, cols]
        prev7 = carry_ref[0, :, cols]
        prev6 = carry_ref[1, :, cols]
        out = []
        for g in range(n_groups):
            yg = y[64 * g:64 * g + 64]
            rot7 = pltpu.roll(yg[56:64], 1, 0)
            rot6 = pltpu.roll(yg[48:56], 1, 0)
            sp1 = jnp.where(sub0, prev7, rot7)
            sp2 = jnp.where(sub0, prev6, rot6)
            y1 = jnp.concatenate([sp1, yg[0:56]], axis=0)
            y2 = jnp.concatenate([sp2, sp1, yg[0:48]], axis=0)
            out.append(cb + w0 * y2 + w1 * y1 + w2 * yg)
            prev7, prev6 = rot7, rot6
        carry_ref[0, :, cols] = prev7
        carry_ref[1, :, cols] = prev6
        return jnp.concatenate(out, axis=0)

    def mlp(t, h):
        def up(c):
            return [jnp.dot(h, wup_ref[:, c0:c0 + FF_CW], preferred_element_type=F32)
                    for c0 in (c * FF_CW, D_FF + c * FF_CW)]
        ys = up(0)
        for c in range(n_chunks):
            nxt = up(c + 1) if c + 1 < n_chunks else None
            ug = conv(ys[0], c * FF_CW)
            uv = conv(ys[1], D_FF + c * FF_CW)
            f_ref[t, :, c * FF_CW:(c + 1) * FF_CW] = (ug * jax.nn.sigmoid(ug) * uv).astype(BF16)
            ys = nxt
        return jnp.dot(f_ref[t], wdn_ref[...], preferred_element_type=F32)

    def post(t, acc):
        rows = slice(t * ts, (t + 1) * ts)
        o_ref[0, rows, :] = (o_ref[0, rows, :]
                             + gt2 * (permute_rows(_rms(acc), 2 + t % 2) * gpo_ref[...]))

    hs = [pre(0)]
    for t in range(n_sub):
        if t + 1 < n_sub:
            hs.append(pre(t + 1))
        acc = mlp(t, hs[t])
        post(t, acc)


def _ffn_call(o_da, o_ret, x, mod, g_post_mix, g_pre_ffn, g_post_ffn, w_out_b, w_up_b,
              conv_w, conv_b, w_down_b):
    b, s, d = x.shape
    tm = TM_FFN
    const = dict(pipeline_mode=pl.Buffered(1))
    vec = lambda n: pl.BlockSpec((1, n), lambda bi, i: (0, 0))
    return pl.pallas_call(
        _ffn_kernel,
        grid=(b, s // tm),
        in_specs=[pl.BlockSpec((1, tm, DA_WIDTH), lambda bi, i: (bi, i, 0)),
                  pl.BlockSpec((1, tm, RET_WIDTH), lambda bi, i: (bi, i, 0)),
                  pl.BlockSpec((1, tm, d), lambda bi, i: (bi, i, 0)),
                  pl.BlockSpec((1, N_MOD, d), lambda bi, i: (bi, 0, 0)),
                  vec(d), vec(d), vec(d),
                  pl.BlockSpec((d, d), lambda bi, i: (0, 0), **const),
                  pl.BlockSpec((d, 2 * D_FF), lambda bi, i: (0, 0), **const),
                  pl.BlockSpec((3, 2 * D_FF), lambda bi, i: (0, 0)),
                  pl.BlockSpec((1, 2 * D_FF), lambda bi, i: (0, 0)),
                  pl.BlockSpec((D_FF, d), lambda bi, i: (0, 0), **const)],
        out_specs=pl.BlockSpec((1, tm, d), lambda bi, i: (bi, i, 0)),
        out_shape=jax.ShapeDtypeStruct((b, s, d), F32),
        scratch_shapes=[pltpu.VMEM((4, d // 128, FFN_SUB, 128), F32),
                        pltpu.VMEM((tm // FFN_SUB, FFN_SUB, D_FF), BF16),
                        pltpu.VMEM((2, 8, 2 * D_FF), F32)],
        compiler_params=pltpu.CompilerParams(dimension_semantics=("arbitrary", "arbitrary"),
                                             vmem_limit_bytes=VMEM_LIMIT_BYTES),
        name="ffn",
    )(o_da, o_ret, x, mod, g_post_mix, g_pre_ffn, g_post_ffn, w_out_b, w_up_b,
      conv_w, conv_b, w_down_b)


def kernel(x, c, w_ada, b_ada, g_pre_mix, w_in, lam_q1, lam_k1, lam_q2, lam_k2, g_da_subln,
           w_out, g_post_mix, g_pre_ffn, w_up, conv_w, conv_b, w_down, g_post_ffn):
    b, s, d = x.shape
    depth = w_ada.shape[0]
    assert depth == 1 and d == D_MODEL and s % TM_IN == 0 and s % TQ == 0
    for l in range(depth):
        mod = _mod_call(c, w_ada[l], b_ada[l]).reshape(b, N_MOD, d)
        w_in_b = w_in[l].astype(BF16)
        w_vt_b = w_in[l][:, 1024:1536].T.astype(BF16)
        qk, vt, o_ret = _inproj_call(x, mod, g_pre_mix[l][None], w_in_b, w_vt_b)
        o_da = _dattn_call(qk, vt, lam_q1[l][None], lam_k1[l][None], lam_q2[l][None],
                           lam_k2[l][None], g_da_subln[l][None])
        x = _ffn_call(o_da, o_ret, x, mod, g_post_mix[l][None], g_pre_ffn[l][None],
                      g_post_ffn[l][None], w_out[l].astype(BF16), w_up[l].astype(BF16),
                      conv_w[l], conv_b[l][None], w_down[l].astype(BF16))
    return x
```

```python
import functools
import math

import jax
import jax.numpy as jnp
from jax import lax
from jax.experimental import pallas as pl
from jax.experimental.pallas import tpu as pltpu

D_MODEL = 1024
CHUNK = 64
DA_HEADS = 4
DA_QK = 64
DA_V = 128
DA_WIDTH = 512
RET_HEADS = 4
RET_QK = 64
RET_V = 128
RET_WIDTH = 512
IN_WIDTH = 3072
D_FF = 2816
N_MOD = 6
RMS_EPS = 1e-6
LAMBDA_INIT = 0.8 - 0.6 * math.exp(-0.3 * 0)

LOG2E = 1.4426950408889634
Q_FOLD = (DA_QK ** -0.5) * LOG2E
RET_K_FOLD = RET_QK ** -0.5

VMEM_LIMIT_BYTES = 56 * 1024 * 1024

BF16 = jnp.bfloat16
F32 = jnp.float32

TM_IN = 512
TQ = 512
TK = 512
RET_L = 256
TM_FFN = 512
FFN_SUB = 256
FF_CW = 256


def _rms(x):
    return x * lax.rsqrt(jnp.mean(x * x, axis=-1, keepdims=True) + RMS_EPS)


def _mod_kernel(c_ref, w_ref, b_ref, o_ref):
    c = c_ref[...]
    a = c * jax.nn.sigmoid(c)
    o_ref[...] = jnp.dot(a.astype(BF16), w_ref[...].astype(BF16),
                         preferred_element_type=F32) + b_ref[...]


def _mod_call(c, w_ada, b_ada):
    b, d = c.shape
    n = w_ada.shape[1]
    tn = 1024
    return pl.pallas_call(
        _mod_kernel,
        grid=(n // tn,),
        in_specs=[pl.BlockSpec((b, d), lambda j: (0, 0)),
                  pl.BlockSpec((d, tn), lambda j: (0, j)),
                  pl.BlockSpec((1, tn), lambda j: (0, j))],
        out_specs=pl.BlockSpec((b, tn), lambda j: (0, j)),
        out_shape=jax.ShapeDtypeStruct((b, n), F32),
        compiler_params=pltpu.CompilerParams(dimension_semantics=("arbitrary",),
                                             vmem_limit_bytes=VMEM_LIMIT_BYTES),
        name="mod",
    )(c, w_ada, b_ada.reshape(1, n))


def _inproj_kernel(x_ref, mod_ref, g_ref, w_ref, qt_ref, k_ref, vt_ref, oret_ref,
                   dec_ref, rowdec_ref, state_ref, wqt_ref, wvt_ref):
    i = pl.program_id(1)
    tm = x_ref.shape[1]
    L = RET_L
    lgs = [math.log(1.0 - 2.0 ** (-5.0 - h)) for h in range(RET_HEADS)]

    @pl.when(i == 0)
    def _():
        state_ref[...] = jnp.zeros_like(state_ref)
        row = lax.broadcasted_iota(jnp.int32, (L, L), 0)
        col = lax.broadcasted_iota(jnp.int32, (L, L), 1)
        d = row - col
        same = (row // CHUNK) == (col // CHUNK)
        past = (col // CHUNK) < (row // CHUNK)
        e = jnp.where(same, jnp.abs(d), d).astype(F32)
        r = lax.broadcasted_iota(jnp.int32, (L, 128), 0).astype(F32)
        for h in range(RET_HEADS):
            dec_ref[h] = jnp.where(same | past, jnp.exp(lgs[h] * e), 0.0)
            rowdec_ref[0, h] = jnp.exp(lgs[h] * (r + 1.0))
            rowdec_ref[1, h] = jnp.exp(lgs[h] * (L - 1.0 - r))

    @pl.when((pl.program_id(0) == 0) & (i == 0))
    def _():
        wqt_ref[...] = w_ref[:, 0:512].astype(F32).T.astype(BF16)
        wvt_ref[...] = w_ref[:, 1024:1536].astype(F32).T.astype(BF16)

    x = x_ref[0]
    sh = mod_ref[0, 0:1, :]
    sc = mod_ref[0, 1:2, :]
    hb = ((_rms(x) * g_ref[...]) * (1.0 + sc) + sh).astype(BF16)

    def proj(lo, hi):
        return jnp.dot(hb, w_ref[:, lo:hi], preferred_element_type=F32)

    rq = proj(1536, 1792).astype(BF16)
    rk = (proj(1792, 2048) * RET_K_FOLD).astype(BF16)
    rv = proj(2048, 2560).astype(BF16)
    rg = proj(2560, 3072)

    lane = lax.broadcasted_iota(jnp.int32, (L, 128), 1)
    nt = (((1,), (1,)), ((), ()))
    blocks = [(t, h) for t in range(tm // L) for h in range(RET_HEADS)]
    qz, kz, vh, sb, kd = {}, {}, {}, {}, {}
    for t, h in blocks:
        rows = slice(t * L, (t + 1) * L)
        pair = slice(128 * (h // 2), 128 * (h // 2) + 128)
        lo = RET_QK * (h % 2)
        sel = (lane >= lo) & (lane < lo + RET_QK)
        qp, kp = rq[rows, pair], rk[rows, pair]
        qz[t, h] = jnp.where(sel, qp, jnp.zeros_like(qp))
        kz[t, h] = jnp.where(sel, kp, jnp.zeros_like(kp))
        vh[t, h] = rv[rows, 128 * h:128 * h + 128]
        s = lax.dot_general(qz[t, h], kz[t, h], nt, preferred_element_type=F32)
        sb[t, h] = (s * dec_ref[h]).astype(BF16)
        kd[t, h] = (kz[t, h].astype(F32) * rowdec_ref[1, h]).astype(BF16)

    def proj_t(wt_ref, out_ref, fold):
        t = lax.dot_general(wt_ref[...], hb, nt, preferred_element_type=F32)
        t = (t if fold is None else t * fold).astype(BF16)
        for j in range(out_ref.shape[1]):
            out_ref[0, j] = t[:, j * TK:(j + 1) * TK]

    proj_t(wqt_ref, qt_ref, Q_FOLD)

    def retention_block(t):
        rows = slice(t * L, (t + 1) * L)
        for h in range(RET_HEADS):
            st = state_ref[h]
            o = jnp.dot(sb[t, h], vh[t, h], preferred_element_type=F32)
            o = o + rowdec_ref[0, h] * jnp.dot(qz[t, h], st.astype(BF16),
                                               preferred_element_type=F32)
            u = lax.dot_general(kd[t, h], vh[t, h], (((0,), (0,)), ((), ())),
                                preferred_element_type=F32)
            state_ref[h] = math.exp(lgs[h] * L) * st + u
            g = rg[rows, 128 * h:128 * h + 128]
            oret_ref[0, rows, 128 * h:128 * h + 128] = (
                _rms(o) * (g * jax.nn.sigmoid(g))).astype(BF16)

    retention_block(0)
    k_ref[0] = proj(512, 1024).astype(BF16)
    for t in range(1, tm // L):
        retention_block(t)
    proj_t(wvt_ref, vt_ref, None)


def _inproj_call(x, mod, g_pre_mix, w_in_b):
    b, s, d = x.shape
    tm = TM_IN
    const = dict(pipeline_mode=pl.Buffered(1))
    return pl.pallas_call(
        _inproj_kernel,
        grid=(b, s // tm),
        in_specs=[pl.BlockSpec((1, tm, d), lambda bi, i: (bi, i, 0)),
                  pl.BlockSpec((1, N_MOD, d), lambda bi, i: (bi, 0, 0)),
                  pl.BlockSpec((1, d), lambda bi, i: (0, 0)),
                  pl.BlockSpec((d, IN_WIDTH), lambda bi, i: (0, 0), **const)],
        out_specs=[pl.BlockSpec((1, tm // TK, DA_WIDTH, TK), lambda bi, i: (bi, i, 0, 0)),
                   pl.BlockSpec((1, tm, DA_WIDTH), lambda bi, i: (bi, i, 0)),
                   pl.BlockSpec((1, tm // TK, DA_WIDTH, TK), lambda bi, i: (bi, i, 0, 0)),
                   pl.BlockSpec((1, tm, RET_WIDTH), lambda bi, i: (bi, i, 0))],
        out_shape=[jax.ShapeDtypeStruct((b, s // TK, DA_WIDTH, TK), BF16),
                   jax.ShapeDtypeStruct((b, s, DA_WIDTH), BF16),
                   jax.ShapeDtypeStruct((b, s // TK, DA_WIDTH, TK), BF16),
                   jax.ShapeDtypeStruct((b, s, RET_WIDTH), BF16)],
        scratch_shapes=[pltpu.VMEM((RET_HEADS, RET_L, RET_L), F32),
                        pltpu.VMEM((2, RET_HEADS, RET_L, 128), F32),
                        pltpu.VMEM((RET_HEADS, 128, RET_V), F32),
                        pltpu.VMEM((DA_WIDTH, d), BF16),
                        pltpu.VMEM((DA_WIDTH, d), BF16)],
        compiler_params=pltpu.CompilerParams(dimension_semantics=("arbitrary", "arbitrary"),
                                             vmem_limit_bytes=VMEM_LIMIT_BYTES),
        name="inproj",
    )(x, mod, g_pre_mix, w_in_b)


def _dattn_kernel(q_ref, k_ref, vt_ref, lq1_ref, lk1_ref, lq2_ref, lk2_ref, gsub_ref,
                  o_ref, dbias_ref, cpos_ref, acc_ref, *bufs):
    tbufs, pbufs = bufs[:DA_HEADS], bufs[DA_HEADS:]
    qi = pl.program_id(1)
    slopes = [LOG2E * 2.0 ** (-8.0 * (h + 1) / DA_HEADS) for h in range(DA_HEADS)]

    @pl.when(qi == 0)
    def _():
        r = lax.broadcasted_iota(jnp.int32, (TK, 128), 0).astype(F32)
        kl = lax.broadcasted_iota(jnp.int32, (TK, 128), 1)
        c = lax.broadcasted_iota(jnp.int32, (TK, TQ), 0)
        a = lax.broadcasted_iota(jnp.int32, (TK, TQ), 1)
        rel = (a - jnp.abs(a - c)).astype(F32)
        allowed = (c // CHUNK) <= (a // CHUNK)
        for h in range(DA_HEADS):
            pbufs[h][...] = jnp.zeros((TK, 2 * TQ), BF16)
            dbias_ref[h] = jnp.where(allowed, slopes[h] * rel, -jnp.inf)
            x = slopes[h] * r
            hi = x.astype(BF16).astype(F32)
            r1 = x - hi
            mid = r1.astype(BF16).astype(F32)
            lo = r1 - mid
            cpos_ref[h] = jnp.where(kl == 0, hi, jnp.where(kl == 1, mid,
                                    jnp.where(kl == 2, lo, 0.0))).astype(BF16)

    ones_rows = jnp.ones((16, TK), BF16)
    bias_rows = jnp.where(lax.broadcasted_iota(jnp.int32, (128, 2 * TQ), 0) < 3,
                          1.0, 0.0).astype(BF16)
    zq = jnp.zeros((DA_QK, TQ), BF16)
    qqt = []
    for h in range(DA_HEADS):
        qt = q_ref[0, 0, 128 * h:128 * h + 128, :]
        qqt.append(jnp.concatenate([jnp.concatenate([qt[0:DA_QK], zq], axis=1),
                                    jnp.concatenate([zq, qt[DA_QK:]], axis=1),
                                    bias_rows], axis=0))

    def kblock(h, kb):
        return k_ref[0, pl.ds(pl.multiple_of(kb * TK, TK), TK), 128 * h:128 * h + 128]

    def diag_scores(h):
        db = dbias_ref[h]
        s = jnp.dot(kblock(h, qi), qqt[h][0:128], preferred_element_type=F32)
        return s + jnp.concatenate([db, db], axis=1)

    def past_scores(h, kb):
        lhs = jnp.concatenate([kblock(h, kb), cpos_ref[h]], axis=1)
        return jnp.dot(lhs, qqt[h], preferred_element_type=F32)

    def score_stage(h, t):
        tbufs[h][...] = t
        return jnp.max(t, axis=0, keepdims=True)

    def softmax_stage(h, tmax, m, shift):
        m_new = jnp.maximum(m, tmax + shift)
        alpha = jnp.exp2(m - m_new)
        pbufs[h][...] = jnp.exp2(tbufs[h][...] - (m_new - shift)).astype(BF16)
        return alpha, m_new

    def value_stage(h, kb, alpha, l, pending=True):
        lhs = jnp.concatenate([vt_ref[0, kb, 128 * h:128 * h + 128, :], ones_rows], axis=0)
        lhs = jnp.where(pending, lhs, jnp.zeros_like(lhs))
        pv = jnp.dot(lhs, pbufs[h][...], preferred_element_type=F32)
        acc_ref[h] = alpha * acc_ref[h] + pv[0:DA_V]
        return alpha * l + pv[DA_V:DA_V + 1]

    heads = range(DA_HEADS)
    for h in heads:
        acc_ref[h] = jnp.zeros((DA_V, 2 * TQ), F32)
    ms = [jnp.full((1, 2 * TQ), -jnp.inf, F32) for _ in heads]
    ls = [jnp.zeros((1, 2 * TQ), F32) for _ in heads]
    alphas = [jnp.ones((1, 2 * TQ), F32) for _ in heads]
    tmaxs = [score_stage(h, diag_scores(h)) for h in heads]

    def step(n, carry):
        ms, ls, alphas, tmaxs = (list(carry[4 * j:4 * j + 4]) for j in range(4))
        kb_value = jnp.where(n == 1, qi, jnp.maximum(n - 2, 0))
        kb_score = jnp.minimum(n, jnp.maximum(qi - 1, 0))
        rel = jnp.where(n == 0, 0, (n - 1 - qi) * TK).astype(F32)
        for h in heads:
            ls[h] = value_stage(h, kb_value, alphas[h], ls[h], pending=n > 0)
        for h in heads:
            alphas[h], ms[h] = softmax_stage(h, tmaxs[h], ms[h], slopes[h] * rel)
            tmaxs[h] = score_stage(h, past_scores(h, kb_score))
        return tuple(ms) + tuple(ls) + tuple(alphas) + tuple(tmaxs)

    carry = lax.fori_loop(0, qi + 1, step, tuple(ms) + tuple(ls) + tuple(alphas) + tuple(tmaxs))
    ms, ls, alphas = (list(carry[4 * j:4 * j + 4]) for j in range(3))
    kb_last = jnp.maximum(qi - 1, 0)
    for h in heads:
        ls[h] = value_stage(h, kb_last, alphas[h], ls[h])
    carry = [x for h in heads for x in (ms[h], ls[h])]

    lam = (jnp.exp(jnp.sum(lq1_ref[...] * lk1_ref[...], axis=-1, keepdims=True))
           - jnp.exp(jnp.sum(lq2_ref[...] * lk2_ref[...], axis=-1, keepdims=True))
           + LAMBDA_INIT)
    for h in range(DA_HEADS):
        inv = 1.0 / carry[2 * h + 1]
        acc = acc_ref[h]
        ot = acc[:, :TQ] * inv[:, :TQ] - lam * (acc[:, TQ:] * inv[:, TQ:])
        ot = ot * lax.rsqrt(jnp.mean(ot * ot, axis=0, keepdims=True) + RMS_EPS)
        o = ot.T * gsub_ref[...] * (1.0 - LAMBDA_INIT)
        o_ref[0, :, 128 * h:128 * h + 128] = o.astype(BF16)


def _dattn_call(qt, k, vt, lam_q1, lam_k1, lam_q2, lam_k2, g_sub):
    b, s, _ = k.shape
    nkv = s // TK
    vec = lambda n: pl.BlockSpec((1, n), lambda bi, i: (0, 0))
    return pl.pallas_call(
        _dattn_kernel,
        grid=(b, s // TQ),
        in_specs=[pl.BlockSpec((1, 1, DA_WIDTH, TQ), lambda bi, i: (bi, i, 0, 0)),
                  pl.BlockSpec((1, s, DA_WIDTH), lambda bi, i: (bi, 0, 0)),
                  pl.BlockSpec((1, nkv, DA_WIDTH, TK), lambda bi, i: (bi, 0, 0, 0)),
                  vec(DA_QK), vec(DA_QK), vec(DA_QK), vec(DA_QK), vec(DA_V)],
        out_specs=pl.BlockSpec((1, TQ, DA_WIDTH), lambda bi, i: (bi, i, 0)),
        out_shape=jax.ShapeDtypeStruct((b, s, DA_WIDTH), BF16),
        scratch_shapes=[pltpu.VMEM((DA_HEADS, TK, TQ), F32),
                        pltpu.VMEM((DA_HEADS, TK, 128), BF16),
                        pltpu.VMEM((DA_HEADS, DA_V, 2 * TQ), F32)]
                       + [pltpu.VMEM((TK, 2 * TQ), F32) for _ in range(DA_HEADS)]
                       + [pltpu.VMEM((TK, 2 * TQ), BF16) for _ in range(DA_HEADS)],
        compiler_params=pltpu.CompilerParams(
            dimension_semantics=("arbitrary", "arbitrary"),
            vmem_limit_bytes=VMEM_LIMIT_BYTES),
        name="dattn",
    )(qt, k, vt, lam_q1, lam_k1, lam_q2, lam_k2, g_sub)


def _ffn_kernel(oda_ref, oret_ref, x_ref, mod_ref, gpm_ref, gpf_ref, gpo_ref,
                wout_ref, wup_ref, cw_ref, cb_ref, wdn_ref, o_ref, perm_ref, f_ref, carry_ref):
    i = pl.program_id(1)
    ts = FFN_SUB
    n_sub = x_ref.shape[1] // ts
    n_groups = ts // 64

    @pl.when(i == 0)
    def _():
        carry_ref[...] = jnp.zeros_like(carry_ref)

    def permute_rows(a, slot):
        n = a.shape[1] // 128
        for s in range(n):
            perm_ref[slot, s] = a[:, 128 * s:128 * s + 128]
        rows = []
        for g in range(n_groups):
            for r in range(8):
                rows.append(jnp.concatenate(
                    [perm_ref[slot, s, pl.ds(64 * g + r, 8, stride=8), :] for s in range(n)],
                    axis=1))
        return jnp.concatenate(rows, axis=0)

    gt1 = mod_ref[0, 2:3, :]
    sh2 = mod_ref[0, 3:4, :]
    sc2 = mod_ref[0, 4:5, :]
    gt2 = mod_ref[0, 5:6, :]
    sub0 = lax.broadcasted_iota(jnp.int32, (8, FF_CW), 0) == 0
    n_chunks = D_FF // FF_CW

    def pre(t):
        rows = slice(t * ts, (t + 1) * ts)
        mix = (jnp.dot(oda_ref[0, rows, :], wout_ref[0:DA_WIDTH, :], preferred_element_type=F32)
               + jnp.dot(oret_ref[0, rows, :], wout_ref[DA_WIDTH:, :],
                         preferred_element_type=F32))
        x1 = x_ref[0, rows, :] + gt1 * (_rms(mix) * gpm_ref[...])
        o_ref[0, rows, :] = x1
        return permute_rows((_rms(x1) * gpf_ref[...]) * (1.0 + sc2) + sh2, t % 2).astype(BF16)

    def conv(y, c0):
        cols = slice(c0, c0 + FF_CW)
        w0, w1, w2, cb = cw_ref[0:1, cols], cw_ref[1:2, cols], cw_ref[2:3, cols], cb_ref[:, cols]
        prev7 = carry_ref[0, :, cols]
        prev6 = carry_ref[1, :, cols]
        out = []
        for g in range(n_groups):
            yg = y[64 * g:64 * g + 64]
            rot7 = pltpu.roll(yg[56:64], 1, 0)
            rot6 = pltpu.roll(yg[48:56], 1, 0)
            sp1 = jnp.where(sub0, prev7, rot7)
            sp2 = jnp.where(sub0, prev6, rot6)
            y1 = jnp.concatenate([sp1, yg[0:56]], axis=0)
            y2 = jnp.concatenate([sp2, sp1, yg[0:48]], axis=0)
            out.append(cb + w0 * y2 + w1 * y1 + w2 * yg)
            prev7, prev6 = rot7, rot6
        carry_ref[0, :, cols] = prev7
        carry_ref[1, :, cols] = prev6
        return jnp.concatenate(out, axis=0)

    def mlp(t, h):
        def up(c):
            return [jnp.dot(h, wup_ref[:, c0:c0 + FF_CW], preferred_element_type=F32)
                    for c0 in (c * FF_CW, D_FF + c * FF_CW)]
        ys = up(0)
        for c in range(n_chunks):
            nxt = up(c + 1) if c + 1 < n_chunks else None
            ug = conv(ys[0], c * FF_CW)
            uv = conv(ys[1], D_FF + c * FF_CW)
            f_ref[t, :, c * FF_CW:(c + 1) * FF_CW] = (ug * jax.nn.sigmoid(ug) * uv).astype(BF16)
            ys = nxt
        return jnp.dot(f_ref[t], wdn_ref[...], preferred_element_type=F32)

    def post(t, acc):
        rows = slice(t * ts, (t + 1) * ts)
        o_ref[0, rows, :] = (o_ref[0, rows, :]
                             + gt2 * (permute_rows(_rms(acc), 2 + t % 2) * gpo_ref[...]))

    hs = [pre(0)]
    for t in range(n_sub):
        if t + 1 < n_sub:
            hs.append(pre(t + 1))
        acc = mlp(t, hs[t])
        post(t, acc)


def _ffn_call(o_da, o_ret, x, mod, g_post_mix, g_pre_ffn, g_post_ffn, w_out_b, w_up_b,
              conv_w, conv_b, w_down_b):
    b, s, d = x.shape
    tm = TM_FFN
    const = dict(pipeline_mode=pl.Buffered(1))
    vec = lambda n: pl.BlockSpec((1, n), lambda bi, i: (0, 0))
    return pl.pallas_call(
        _ffn_kernel,
        grid=(b, s // tm),
        in_specs=[pl.BlockSpec((1, tm, DA_WIDTH), lambda bi, i: (bi, i, 0)),
                  pl.BlockSpec((1, tm, RET_WIDTH), lambda bi, i: (bi, i, 0)),
                  pl.BlockSpec((1, tm, d), lambda bi, i: (bi, i, 0)),
                  pl.BlockSpec((1, N_MOD, d), lambda bi, i: (bi, 0, 0)),
                  vec(d), vec(d), vec(d),
                  pl.BlockSpec((d, d), lambda bi, i: (0, 0), **const),
                  pl.BlockSpec((d, 2 * D_FF), lambda bi, i: (0, 0), **const),
                  pl.BlockSpec((3, 2 * D_FF), lambda bi, i: (0, 0)),
                  pl.BlockSpec((1, 2 * D_FF), lambda bi, i: (0, 0)),
                  pl.BlockSpec((D_FF, d), lambda bi, i: (0, 0), **const)],
        out_specs=pl.BlockSpec((1, tm, d), lambda bi, i: (bi, i, 0)),
        out_shape=jax.ShapeDtypeStruct((b, s, d), F32),
        scratch_shapes=[pltpu.VMEM((4, d // 128, FFN_SUB, 128), F32),
                        pltpu.VMEM((tm // FFN_SUB, FFN_SUB, D_FF), BF16),
                        pltpu.VMEM((2, 8, 2 * D_FF), F32)],
        compiler_params=pltpu.CompilerParams(dimension_semantics=("arbitrary", "arbitrary"),
                                             vmem_limit_bytes=VMEM_LIMIT_BYTES),
        name="ffn",
    )(o_da, o_ret, x, mod, g_post_mix, g_pre_ffn, g_post_ffn, w_out_b, w_up_b,
      conv_w, conv_b, w_down_b)


def kernel(x, c, w_ada, b_ada, g_pre_mix, w_in, lam_q1, lam_k1, lam_q2, lam_k2, g_da_subln,
           w_out, g_post_mix, g_pre_ffn, w_up, conv_w, conv_b, w_down, g_post_ffn):
    b, s, d = x.shape
    depth = w_ada.shape[0]
    assert depth == 1 and d == D_MODEL and s % TM_IN == 0 and s % TQ == 0
    for l in range(depth):
        mod = _mod_call(c, w_ada[l], b_ada[l]).reshape(b, N_MOD, d)
        w_in_b = w_in[l].astype(BF16)
        qt, k, vt, o_ret = _inproj_call(x, mod, g_pre_mix[l][None], w_in_b)
        o_da = _dattn_call(qt, k, vt, lam_q1[l][None], lam_k1[l][None], lam_q2[l][None],
                           lam_k2[l][None], g_da_subln[l][None])
        x = _ffn_call(o_da, o_ret, x, mod, g_post_mix[l][None], g_pre_ffn[l][None],
                      g_post_ffn[l][None], w_out[l].astype(BF16), w_up[l].astype(BF16),
                      conv_w[l], conv_b[l][None], w_down[l].astype(BF16))
    return x
```

```python
import functools
import math

import jax
import jax.numpy as jnp
from jax import lax
from jax.experimental import pallas as pl
from jax.experimental.pallas import tpu as pltpu

D_MODEL = 1024
CHUNK = 64
DA_HEADS = 4
DA_QK = 64
DA_V = 128
DA_WIDTH = 512
RET_HEADS = 4
RET_QK = 64
RET_V = 128
RET_WIDTH = 512
IN_WIDTH = 3072
D_FF = 2816
N_MOD = 6
RMS_EPS = 1e-6
LAMBDA_INIT = 0.8 - 0.6 * math.exp(-0.3 * 0)

LOG2E = 1.4426950408889634
Q_FOLD = (DA_QK ** -0.5) * LOG2E
RET_K_FOLD = RET_QK ** -0.5

VMEM_LIMIT_BYTES = 56 * 1024 * 1024

BF16 = jnp.bfloat16
F32 = jnp.float32

TM_IN = 512
TQ = 512
TK = 512
RET_L = 256
TM_FFN = 512
FFN_SUBS = (192, 320)
FF_CW = 256


def _rms(x):
    return x * lax.rsqrt(jnp.mean(x * x, axis=-1, keepdims=True) + RMS_EPS)


def _mod_kernel(c_ref, w_ref, b_ref, o_ref):
    c = c_ref[...]
    a = c * jax.nn.sigmoid(c)
    o_ref[...] = jnp.dot(a.astype(BF16), w_ref[...].astype(BF16),
                         preferred_element_type=F32) + b_ref[...]


def _mod_call(c, w_ada, b_ada):
    b, d = c.shape
    n = w_ada.shape[1]
    tn = 1024
    return pl.pallas_call(
        _mod_kernel,
        grid=(n // tn,),
        in_specs=[pl.BlockSpec((b, d), lambda j: (0, 0)),
                  pl.BlockSpec((d, tn), lambda j: (0, j)),
                  pl.BlockSpec((1, tn), lambda j: (0, j))],
        out_specs=pl.BlockSpec((b, tn), lambda j: (0, j)),
        out_shape=jax.ShapeDtypeStruct((b, n), F32),
        compiler_params=pltpu.CompilerParams(dimension_semantics=("arbitrary",),
                                             vmem_limit_bytes=VMEM_LIMIT_BYTES),
        name="mod",
    )(c, w_ada, b_ada.reshape(1, n))


def _inproj_kernel(x_ref, mod_ref, g_ref, w_ref, qt_ref, k_ref, vt_ref, oret_ref,
                   dec_ref, rowdec_ref, state_ref, wqt_ref, wvt_ref):
    i = pl.program_id(1)
    tm = x_ref.shape[1]
    L = RET_L
    lgs = [math.log(1.0 - 2.0 ** (-5.0 - h)) for h in range(RET_HEADS)]

    @pl.when(i == 0)
    def _():
        state_ref[...] = jnp.zeros_like(state_ref)
        row = lax.broadcasted_iota(jnp.int32, (L, L), 0)
        col = lax.broadcasted_iota(jnp.int32, (L, L), 1)
        d = row - col
        same = (row // CHUNK) == (col // CHUNK)
        past = (col // CHUNK) < (row // CHUNK)
        e = jnp.where(same, jnp.abs(d), d).astype(F32)
        r = lax.broadcasted_iota(jnp.int32, (L, 128), 0).astype(F32)
        for h in range(RET_HEADS):
            dec_ref[h] = jnp.where(same | past, jnp.exp(lgs[h] * e), 0.0)
            rowdec_ref[0, h] = jnp.exp(lgs[h] * (r + 1.0))
            rowdec_ref[1, h] = jnp.exp(lgs[h] * (L - 1.0 - r))

    @pl.when((pl.program_id(0) == 0) & (i == 0))
    def _():
        wqt_ref[...] = w_ref[:, 0:512].astype(F32).T.astype(BF16)
        wvt_ref[...] = w_ref[:, 1024:1536].astype(F32).T.astype(BF16)

    x = x_ref[0]
    sh = mod_ref[0, 0:1, :]
    sc = mod_ref[0, 1:2, :]
    hb = (_rms(x) * (g_ref[...] * (1.0 + sc)) + sh).astype(BF16)

    def proj(lo, hi):
        return jnp.dot(hb, w_ref[:, lo:hi], preferred_element_type=F32)

    rq = proj(1536, 1792).astype(BF16)
    rk = (proj(1792, 2048) * RET_K_FOLD).astype(BF16)
    rv = proj(2048, 2560).astype(BF16)
    rg = proj(2560, 3072)

    lane = lax.broadcasted_iota(jnp.int32, (L, 128), 1)
    nt = (((1,), (1,)), ((), ()))
    blocks = [(t, h) for t in range(tm // L) for h in range(RET_HEADS)]
    qz, kz, vh, sb, kd = {}, {}, {}, {}, {}
    for t, h in blocks:
        rows = slice(t * L, (t + 1) * L)
        pair = slice(128 * (h // 2), 128 * (h // 2) + 128)
        lo = RET_QK * (h % 2)
        sel = (lane >= lo) & (lane < lo + RET_QK)
        qp, kp = rq[rows, pair], rk[rows, pair]
        qz[t, h] = jnp.where(sel, qp, jnp.zeros_like(qp))
        kz[t, h] = jnp.where(sel, kp, jnp.zeros_like(kp))
        vh[t, h] = rv[rows, 128 * h:128 * h + 128]
        s = lax.dot_general(qz[t, h], kz[t, h], nt, preferred_element_type=F32)
        sb[t, h] = (s * dec_ref[h]).astype(BF16)
        kd[t, h] = (kz[t, h].astype(F32) * rowdec_ref[1, h]).astype(BF16)

    def proj_t(wt_ref, out_ref, fold):
        t = lax.dot_general(wt_ref[...], hb, nt, preferred_element_type=F32)
        t = (t if fold is None else t * fold).astype(BF16)
        for j in range(out_ref.shape[1]):
            out_ref[0, j] = t[:, j * TK:(j + 1) * TK]

    proj_t(wqt_ref, qt_ref, Q_FOLD)

    def retention_block(t):
        rows = slice(t * L, (t + 1) * L)
        for h in range(RET_HEADS):
            st = state_ref[h]
            o = jnp.dot(sb[t, h], vh[t, h], preferred_element_type=F32)
            o = o + rowdec_ref[0, h] * jnp.dot(qz[t, h], st.astype(BF16),
                                               preferred_element_type=F32)
            u = lax.dot_general(kd[t, h], vh[t, h], (((0,), (0,)), ((), ())),
                                preferred_element_type=F32)
            state_ref[h] = math.exp(lgs[h] * L) * st + u
            g = rg[rows, 128 * h:128 * h + 128]
            oret_ref[0, rows, 128 * h:128 * h + 128] = (
                _rms(o) * (g * jax.nn.sigmoid(g))).astype(BF16)

    retention_block(0)
    k_ref[0] = proj(512, 1024).astype(BF16)
    for t in range(1, tm // L):
        retention_block(t)
    proj_t(wvt_ref, vt_ref, None)


def _inproj_call(x, mod, g_pre_mix, w_in_b):
    b, s, d = x.shape
    tm = TM_IN
    const = dict(pipeline_mode=pl.Buffered(1))
    return pl.pallas_call(
        _inproj_kernel,
        grid=(b, s // tm),
        in_specs=[pl.BlockSpec((1, tm, d), lambda bi, i: (bi, i, 0)),
                  pl.BlockSpec((1, N_MOD, d), lambda bi, i: (bi, 0, 0)),
                  pl.BlockSpec((1, d), lambda bi, i: (0, 0)),
                  pl.BlockSpec((d, IN_WIDTH), lambda bi, i: (0, 0), **const)],
        out_specs=[pl.BlockSpec((1, tm // TK, DA_WIDTH, TK), lambda bi, i: (bi, i, 0, 0)),
                   pl.BlockSpec((1, tm, DA_WIDTH), lambda bi, i: (bi, i, 0)),
                   pl.BlockSpec((1, tm // TK, DA_WIDTH, TK), lambda bi, i: (bi, i, 0, 0)),
                   pl.BlockSpec((1, tm, RET_WIDTH), lambda bi, i: (bi, i, 0))],
        out_shape=[jax.ShapeDtypeStruct((b, s // TK, DA_WIDTH, TK), BF16),
                   jax.ShapeDtypeStruct((b, s, DA_WIDTH), BF16),
                   jax.ShapeDtypeStruct((b, s // TK, DA_WIDTH, TK), BF16),
                   jax.ShapeDtypeStruct((b, s, RET_WIDTH), BF16)],
        scratch_shapes=[pltpu.VMEM((RET_HEADS, RET_L, RET_L), F32),
                        pltpu.VMEM((2, RET_HEADS, RET_L, 128), F32),
                        pltpu.VMEM((RET_HEADS, 128, RET_V), F32),
                        pltpu.VMEM((DA_WIDTH, d), BF16),
                        pltpu.VMEM((DA_WIDTH, d), BF16)],
        compiler_params=pltpu.CompilerParams(dimension_semantics=("arbitrary", "arbitrary"),
                                             vmem_limit_bytes=VMEM_LIMIT_BYTES),
        name="inproj",
    )(x, mod, g_pre_mix, w_in_b)


def _dattn_kernel(q_ref, k_ref, vt_ref, lq1_ref, lk1_ref, lq2_ref, lk2_ref, gsub_ref,
                  o_ref, dbias_ref, cpos_ref, acc_ref, *bufs):
    tbufs, pbufs = bufs[:DA_HEADS], bufs[DA_HEADS:]
    qi = pl.program_id(1)
    slopes = [LOG2E * 2.0 ** (-8.0 * (h + 1) / DA_HEADS) for h in range(DA_HEADS)]

    @pl.when(qi == 0)
    def _():
        r = lax.broadcasted_iota(jnp.int32, (TK, 128), 0).astype(F32)
        kl = lax.broadcasted_iota(jnp.int32, (TK, 128), 1)
        c = lax.broadcasted_iota(jnp.int32, (TK, TQ), 0)
        a = lax.broadcasted_iota(jnp.int32, (TK, TQ), 1)
        rel = (a - jnp.abs(a - c)).astype(F32)
        allowed = (c // CHUNK) <= (a // CHUNK)
        for h in range(DA_HEADS):
            pbufs[h][...] = jnp.zeros((TK, 2 * TQ), BF16)
            dbias_ref[h] = jnp.where(allowed, slopes[h] * rel, -jnp.inf)
            x = slopes[h] * r
            hi = x.astype(BF16).astype(F32)
            r1 = x - hi
            mid = r1.astype(BF16).astype(F32)
            lo = r1 - mid
            cpos_ref[h] = jnp.where(kl == 0, hi, jnp.where(kl == 1, mid,
                                    jnp.where(kl == 2, lo, 0.0))).astype(BF16)

    ones_rows = jnp.ones((16, TK), BF16)
    bias_rows = jnp.where(lax.broadcasted_iota(jnp.int32, (128, 2 * TQ), 0) < 3,
                          1.0, 0.0).astype(BF16)
    zq = jnp.zeros((DA_QK, TQ), BF16)
    qqt = []
    for h in range(DA_HEADS):
        qt = q_ref[0, 0, 128 * h:128 * h + 128, :]
        qqt.append(jnp.concatenate([jnp.concatenate([qt[0:DA_QK], zq], axis=1),
                                    jnp.concatenate([zq, qt[DA_QK:]], axis=1),
                                    bias_rows], axis=0))

    def kblock(h, kb):
        return k_ref[0, pl.ds(pl.multiple_of(kb * TK, TK), TK), 128 * h:128 * h + 128]

    def diag_scores(h):
        db = dbias_ref[h]
        s = jnp.dot(kblock(h, qi), qqt[h][0:128], preferred_element_type=F32)
        return s + jnp.concatenate([db, db], axis=1)

    def past_scores(h, kb):
        lhs = jnp.concatenate([kblock(h, kb), cpos_ref[h]], axis=1)
        return jnp.dot(lhs, qqt[h], preferred_element_type=F32)

    def score_stage(h, t):
        tbufs[h][...] = t
        return jnp.max(t, axis=0, keepdims=True)

    def softmax_stage(h, tmax, m, shift):
        m_new = jnp.maximum(m, tmax + shift)
        alpha = jnp.exp2(m - m_new)
        pbufs[h][...] = jnp.exp2(tbufs[h][...] - (m_new - shift)).astype(BF16)
        return alpha, m_new

    def value_stage(h, kb, alpha, l, pending=True):
        lhs = jnp.concatenate([vt_ref[0, kb, 128 * h:128 * h + 128, :], ones_rows], axis=0)
        lhs = jnp.where(pending, lhs, jnp.zeros_like(lhs))
        pv = jnp.dot(lhs, pbufs[h][...], preferred_element_type=F32)
        acc_ref[h] = alpha * acc_ref[h] + pv[0:DA_V]
        return alpha * l + pv[DA_V:DA_V + 1]

    heads = range(DA_HEADS)
    for h in heads:
        acc_ref[h] = jnp.zeros((DA_V, 2 * TQ), F32)
    ms = [jnp.full((1, 2 * TQ), -jnp.inf, F32) for _ in heads]
    ls = [jnp.zeros((1, 2 * TQ), F32) for _ in heads]
    alphas = [jnp.ones((1, 2 * TQ), F32) for _ in heads]
    tmaxs = [score_stage(h, diag_scores(h)) for h in heads]

    def step(n, carry):
        ms, ls, alphas, tmaxs = (list(carry[4 * j:4 * j + 4]) for j in range(4))
        kb_value = jnp.where(n == 1, qi, jnp.maximum(n - 2, 0))
        kb_score = jnp.minimum(n, jnp.maximum(qi - 1, 0))
        rel = jnp.where(n == 0, 0, (n - 1 - qi) * TK).astype(F32)
        for h in heads:
            ls[h] = value_stage(h, kb_value, alphas[h], ls[h], pending=n > 0)
        for h in heads:
            alphas[h], ms[h] = softmax_stage(h, tmaxs[h], ms[h], slopes[h] * rel)
            tmaxs[h] = score_stage(h, past_scores(h, kb_score))
        return tuple(ms) + tuple(ls) + tuple(alphas) + tuple(tmaxs)

    carry = lax.fori_loop(0, qi + 1, step, tuple(ms) + tuple(ls) + tuple(alphas) + tuple(tmaxs))
    ms, ls, alphas = (list(carry[4 * j:4 * j + 4]) for j in range(3))
    kb_last = jnp.maximum(qi - 1, 0)
    for h in heads:
        ls[h] = value_stage(h, kb_last, alphas[h], ls[h])
    carry = [x for h in heads for x in (ms[h], ls[h])]

    lam = (jnp.exp(jnp.sum(lq1_ref[...] * lk1_ref[...], axis=-1, keepdims=True))
           - jnp.exp(jnp.sum(lq2_ref[...] * lk2_ref[...], axis=-1, keepdims=True))
           + LAMBDA_INIT)
    for h in range(DA_HEADS):
        inv = 1.0 / carry[2 * h + 1]
        acc = acc_ref[h]
        ot = acc[:, :TQ] * inv[:, :TQ] - lam * (acc[:, TQ:] * inv[:, TQ:])
        ot = ot * lax.rsqrt(jnp.mean(ot * ot, axis=0, keepdims=True) + RMS_EPS)
        o = ot.T * (gsub_ref[...] * (1.0 - LAMBDA_INIT))
        o_ref[0, :, 128 * h:128 * h + 128] = o.astype(BF16)


def _dattn_call(qt, k, vt, lam_q1, lam_k1, lam_q2, lam_k2, g_sub):
    b, s, _ = k.shape
    nkv = s // TK
    vec = lambda n: pl.BlockSpec((1, n), lambda bi, i: (0, 0))
    return pl.pallas_call(
        _dattn_kernel,
        grid=(b, s // TQ),
        in_specs=[pl.BlockSpec((1, 1, DA_WIDTH, TQ), lambda bi, i: (bi, i, 0, 0)),
                  pl.BlockSpec((1, s, DA_WIDTH), lambda bi, i: (bi, 0, 0)),
                  pl.BlockSpec((1, nkv, DA_WIDTH, TK), lambda bi, i: (bi, 0, 0, 0)),
                  vec(DA_QK), vec(DA_QK), vec(DA_QK), vec(DA_QK), vec(DA_V)],
        out_specs=pl.BlockSpec((1, TQ, DA_WIDTH), lambda bi, i: (bi, i, 0)),
        out_shape=jax.ShapeDtypeStruct((b, s, DA_WIDTH), BF16),
        scratch_shapes=[pltpu.VMEM((DA_HEADS, TK, TQ), F32),
                        pltpu.VMEM((DA_HEADS, TK, 128), BF16),
                        pltpu.VMEM((DA_HEADS, DA_V, 2 * TQ), F32)]
                       + [pltpu.VMEM((TK, 2 * TQ), F32) for _ in range(DA_HEADS)]
                       + [pltpu.VMEM((TK, 2 * TQ), BF16) for _ in range(DA_HEADS)],
        compiler_params=pltpu.CompilerParams(
            dimension_semantics=("arbitrary", "arbitrary"),
            vmem_limit_bytes=VMEM_LIMIT_BYTES),
        name="dattn",
    )(qt, k, vt, lam_q1, lam_k1, lam_q2, lam_k2, g_sub)


def _ffn_kernel(oda_ref, oret_ref, x_ref, mod_ref, gpm_ref, gpf_ref, gpo_ref,
                wout_ref, wup_ref, cw_ref, cb_ref, wdn_ref, o_ref, perm_ref, f_ref, carry_ref):
    i = pl.program_id(1)
    n_sub = len(FFN_SUBS)
    starts = [sum(FFN_SUBS[:t]) for t in range(n_sub)]
    assert sum(FFN_SUBS) == x_ref.shape[1]

    def sub_rows(t):
        return slice(starts[t], starts[t] + FFN_SUBS[t])

    @pl.when(i == 0)
    def _():
        carry_ref[...] = jnp.zeros_like(carry_ref)

    def permute_rows(a, slot):
        n = a.shape[1] // 128
        for s in range(n):
            perm_ref[slot, s, 0:a.shape[0], :] = a[:, 128 * s:128 * s + 128]
        rows = []
        for g in range(a.shape[0] // 64):
            for r in range(8):
                rows.append(jnp.concatenate(
                    [perm_ref[slot, s, pl.ds(64 * g + r, 8, stride=8), :] for s in range(n)],
                    axis=1))
        return jnp.concatenate(rows, axis=0)

    gt1 = mod_ref[0, 2:3, :]
    sh2 = mod_ref[0, 3:4, :]
    sc2 = mod_ref[0, 4:5, :]
    gt2 = mod_ref[0, 5:6, :]
    sub0 = lax.broadcasted_iota(jnp.int32, (8, FF_CW), 0) == 0
    n_chunks = D_FF // FF_CW

    def pre(t):
        rows = sub_rows(t)
        mix = (jnp.dot(oda_ref[0, rows, :], wout_ref[0:DA_WIDTH, :], preferred_element_type=F32)
               + jnp.dot(oret_ref[0, rows, :], wout_ref[DA_WIDTH:, :],
                         preferred_element_type=F32))
        x1 = x_ref[0, rows, :] + _rms(mix) * (gt1 * gpm_ref[...])
        o_ref[0, rows, :] = x1
        return permute_rows(_rms(x1) * (gpf_ref[...] * (1.0 + sc2)) + sh2, t % 2).astype(BF16)

    def conv(y, c0):
        cols = slice(c0, c0 + FF_CW)
        w0, w1, w2, cb = cw_ref[0:1, cols], cw_ref[1:2, cols], cw_ref[2:3, cols], cb_ref[:, cols]
        prev7 = carry_ref[0, :, cols]
        prev6 = carry_ref[1, :, cols]
        out = []
        for g in range(y.shape[0] // 64):
            yg = y[64 * g:64 * g + 64]
            rot7 = pltpu.roll(yg[56:64], 1, 0)
            rot6 = pltpu.roll(yg[48:56], 1, 0)
            sp1 = jnp.where(sub0, prev7, rot7)
            sp2 = jnp.where(sub0, prev6, rot6)
            y1 = jnp.concatenate([sp1, yg[0:56]], axis=0)
            y2 = jnp.concatenate([sp2, sp1, yg[0:48]], axis=0)
            out.append(cb + w0 * y2 + w1 * y1 + w2 * yg)
            prev7, prev6 = rot7, rot6
        carry_ref[0, :, cols] = prev7
        carry_ref[1, :, cols] = prev6
        return jnp.concatenate(out, axis=0)

    def mlp(t, h):
        def up(c):
            return [jnp.dot(h, wup_ref[:, c0:c0 + FF_CW], preferred_element_type=F32)
                    for c0 in (c * FF_CW, D_FF + c * FF_CW)]
        ys = up(0)
        for c in range(n_chunks):
            nxt = up(c + 1) if c + 1 < n_chunks else None
            ug = conv(ys[0], c * FF_CW)
            uv = conv(ys[1], D_FF + c * FF_CW)
            f_ref[sub_rows(t), c * FF_CW:(c + 1) * FF_CW] = (
                ug * jax.nn.sigmoid(ug) * uv).astype(BF16)
            ys = nxt
        return jnp.dot(f_ref[sub_rows(t), :], wdn_ref[...], preferred_element_type=F32)

    def post(t, acc):
        rows = sub_rows(t)
        o_ref[0, rows, :] = (o_ref[0, rows, :]
                             + permute_rows(_rms(acc), 2 + t % 2) * (gt2 * gpo_ref[...]))

    hs = [pre(0)]
    for t in range(n_sub):
        if t + 1 < n_sub:
            hs.append(pre(t + 1))
        acc = mlp(t, hs[t])
        post(t, acc)


def _ffn_call(o_da, o_ret, x, mod, g_post_mix, g_pre_ffn, g_post_ffn, w_out_b, w_up_b,
              conv_w, conv_b, w_down_b):
    b, s, d = x.shape
    tm = TM_FFN
    const = dict(pipeline_mode=pl.Buffered(1))
    vec = lambda n: pl.BlockSpec((1, n), lambda bi, i: (0, 0))
    return pl.pallas_call(
        _ffn_kernel,
        grid=(b, s // tm),
        in_specs=[pl.BlockSpec((1, tm, DA_WIDTH), lambda bi, i: (bi, i, 0)),
                  pl.BlockSpec((1, tm, RET_WIDTH), lambda bi, i: (bi, i, 0)),
                  pl.BlockSpec((1, tm, d), lambda bi, i: (bi, i, 0)),
                  pl.BlockSpec((1, N_MOD, d), lambda bi, i: (bi, 0, 0)),
                  vec(d), vec(d), vec(d),
                  pl.BlockSpec((d, d), lambda bi, i: (0, 0), **const),
                  pl.BlockSpec((d, 2 * D_FF), lambda bi, i: (0, 0), **const),
                  pl.BlockSpec((3, 2 * D_FF), lambda bi, i: (0, 0)),
                  pl.BlockSpec((1, 2 * D_FF), lambda bi, i: (0, 0)),
                  pl.BlockSpec((D_FF, d), lambda bi, i: (0, 0), **const)],
        out_specs=pl.BlockSpec((1, tm, d), lambda bi, i: (bi, i, 0)),
        out_shape=jax.ShapeDtypeStruct((b, s, d), F32),
        scratch_shapes=[pltpu.VMEM((4, d // 128, max(FFN_SUBS), 128), F32),
                        pltpu.VMEM((tm, D_FF), BF16),
                        pltpu.VMEM((2, 8, 2 * D_FF), F32)],
        compiler_params=pltpu.CompilerParams(dimension_semantics=("arbitrary", "arbitrary"),
                                             vmem_limit_bytes=VMEM_LIMIT_BYTES),
        name="ffn",
    )(o_da, o_ret, x, mod, g_post_mix, g_pre_ffn, g_post_ffn, w_out_b, w_up_b,
      conv_w, conv_b, w_down_b)


def kernel(x, c, w_ada, b_ada, g_pre_mix, w_in, lam_q1, lam_k1, lam_q2, lam_k2, g_da_subln,
           w_out, g_post_mix, g_pre_ffn, w_up, conv_w, conv_b, w_down, g_post_ffn):
    b, s, d = x.shape
    depth = w_ada.shape[0]
    assert depth == 1 and d == D_MODEL and s % TM_IN == 0 and s % TQ == 0
    for l in range(depth):
        mod = _mod_call(c, w_ada[l], b_ada[l]).reshape(b, N_MOD, d)
        w_in_b = w_in[l].astype(BF16)
        qt, k, vt, o_ret = _inproj_call(x, mod, g_pre_mix[l][None], w_in_b)
        o_da = _dattn_call(qt, k, vt, lam_q1[l][None], lam_k1[l][None], lam_q2[l][None],
                           lam_k2[l][None], g_da_subln[l][None])
        x = _ffn_call(o_da, o_ret, x, mod, g_post_mix[l][None], g_pre_ffn[l][None],
                      g_post_ffn[l][None], w_out[l].astype(BF16), w_up[l].astype(BF16),
                      conv_w[l], conv_b[l][None], w_down[l].astype(BF16))
    return x
```

```python
import functools
import math

import jax
import jax.numpy as jnp
from jax import lax
from jax.experimental import pallas as pl
from jax.experimental.pallas import tpu as pltpu

D_MODEL = 1024
CHUNK = 64
DA_HEADS = 4
DA_QK = 64
DA_V = 128
DA_WIDTH = 512
RET_HEADS = 4
RET_QK = 64
RET_V = 128
RET_WIDTH = 512
IN_WIDTH = 3072
D_FF = 2816
N_MOD = 6
RMS_EPS = 1e-6
LAMBDA_INIT = 0.8 - 0.6 * math.exp(-0.3 * 0)

LOG2E = 1.4426950408889634
Q_FOLD = (DA_QK ** -0.5) * LOG2E
RET_K_FOLD = RET_QK ** -0.5

VMEM_LIMIT_BYTES = 56 * 1024 * 1024

BF16 = jnp.bfloat16
F32 = jnp.float32

TM_IN = 1024
TQ = 512
TK = 512
RET_L = 256
TM_FFN = 512
FFN_SUBS = (192, 320)
FF_CW = 256


def _rms(x):
    return x * lax.rsqrt(jnp.mean(x * x, axis=-1, keepdims=True) + RMS_EPS)


def _mod_kernel(c_ref, w_ref, b_ref, o_ref):
    c = c_ref[...]
    a = c * jax.nn.sigmoid(c)
    o_ref[...] = jnp.dot(a.astype(BF16), w_ref[...].astype(BF16),
                         preferred_element_type=F32) + b_ref[...]


def _mod_call(c, w_ada, b_ada):
    b, d = c.shape
    n = w_ada.shape[1]
    tn = 1024
    return pl.pallas_call(
        _mod_kernel,
        grid=(n // tn,),
        in_specs=[pl.BlockSpec((b, d), lambda j: (0, 0)),
                  pl.BlockSpec((d, tn), lambda j: (0, j)),
                  pl.BlockSpec((1, tn), lambda j: (0, j))],
        out_specs=pl.BlockSpec((b, tn), lambda j: (0, j)),
        out_shape=jax.ShapeDtypeStruct((b, n), F32),
        compiler_params=pltpu.CompilerParams(dimension_semantics=("arbitrary",),
                                             vmem_limit_bytes=VMEM_LIMIT_BYTES),
        name="mod",
    )(c, w_ada, b_ada.reshape(1, n))


def _inproj_kernel(x_ref, mod_ref, g_ref, w_ref, qt_ref, k_ref, vt_ref, oret_ref,
                   dec_ref, rowdec_ref, state_ref, wqt_ref, wvt_ref):
    i = pl.program_id(1)
    tm = x_ref.shape[1]
    L = RET_L
    lgs = [math.log(1.0 - 2.0 ** (-5.0 - h)) for h in range(RET_HEADS)]

    @pl.when(i == 0)
    def _():
        state_ref[...] = jnp.zeros_like(state_ref)

    @pl.when((pl.program_id(0) == 0) & (i == 0))
    def _():
        row = lax.broadcasted_iota(jnp.int32, (L, L), 0)
        col = lax.broadcasted_iota(jnp.int32, (L, L), 1)
        d = row - col
        same = (row // CHUNK) == (col // CHUNK)
        past = (col // CHUNK) < (row // CHUNK)
        e = jnp.where(same, jnp.abs(d), d).astype(F32)
        r = lax.broadcasted_iota(jnp.int32, (L, 128), 0).astype(F32)
        for h in range(RET_HEADS):
            dec_ref[h] = jnp.where(same | past, jnp.exp(lgs[h] * e), 0.0)
            rowdec_ref[0, h] = jnp.exp(lgs[h] * (r + 1.0))
            rowdec_ref[1, h] = jnp.exp(lgs[h] * (L - 1.0 - r))
        wqt_ref[...] = w_ref[:, 0:512].astype(F32).T.astype(BF16)
        wvt_ref[...] = w_ref[:, 1024:1536].astype(F32).T.astype(BF16)

    x = x_ref[0]
    sh = mod_ref[0, 0:1, :]
    sc = mod_ref[0, 1:2, :]
    hb = (_rms(x) * (g_ref[...] * (1.0 + sc)) + sh).astype(BF16)

    def proj(lo, hi):
        return jnp.dot(hb, w_ref[:, lo:hi], preferred_element_type=F32)

    rq = proj(1536, 1792).astype(BF16)
    rk = (proj(1792, 2048) * RET_K_FOLD).astype(BF16)
    rv = proj(2048, 2560).astype(BF16)
    rg = proj(2560, 3072)

    lane = lax.broadcasted_iota(jnp.int32, (L, 128), 1)
    nt = (((1,), (1,)), ((), ()))
    blocks = [(t, h) for t in range(tm // L) for h in range(RET_HEADS)]
    qz, kz, vh, sb, kd = {}, {}, {}, {}, {}
    for t, h in blocks:
        rows = slice(t * L, (t + 1) * L)
        pair = slice(128 * (h // 2), 128 * (h // 2) + 128)
        lo = RET_QK * (h % 2)
        sel = (lane >= lo) & (lane < lo + RET_QK)
        qp, kp = rq[rows, pair], rk[rows, pair]
        qz[t, h] = jnp.where(sel, qp, jnp.zeros_like(qp))
        kz[t, h] = jnp.where(sel, kp, jnp.zeros_like(kp))
        vh[t, h] = rv[rows, 128 * h:128 * h + 128]
        s = lax.dot_general(qz[t, h], kz[t, h], nt, preferred_element_type=F32)
        sb[t, h] = (s * dec_ref[h]).astype(BF16)
        kd[t, h] = (kz[t, h].astype(F32) * rowdec_ref[1, h]).astype(BF16)

    def proj_t(wt_ref, out_ref, fold):
        t = lax.dot_general(wt_ref[...], hb, nt, preferred_element_type=F32)
        t = (t if fold is None else t * fold).astype(BF16)
        for j in range(out_ref.shape[1]):
            out_ref[0, j] = t[:, j * TK:(j + 1) * TK]

    proj_t(wqt_ref, qt_ref, Q_FOLD)

    def retention_block(t):
        rows = slice(t * L, (t + 1) * L)
        for h in range(RET_HEADS):
            st = state_ref[h]
            o = jnp.dot(sb[t, h], vh[t, h], preferred_element_type=F32)
            o = o + rowdec_ref[0, h] * jnp.dot(qz[t, h], st.astype(BF16),
                                               preferred_element_type=F32)
            u = lax.dot_general(kd[t, h], vh[t, h], (((0,), (0,)), ((), ())),
                                preferred_element_type=F32)
            state_ref[h] = math.exp(lgs[h] * L) * st + u
            g = rg[rows, 128 * h:128 * h + 128]
            oret_ref[0, rows, 128 * h:128 * h + 128] = (
                _rms(o) * (g * jax.nn.sigmoid(g))).astype(BF16)

    retention_block(0)
    k_ref[0] = proj(512, 1024).astype(BF16)
    for t in range(1, tm // L):
        retention_block(t)
    proj_t(wvt_ref, vt_ref, None)


def _inproj_call(x, mod, g_pre_mix, w_in_b):
    b, s, d = x.shape
    tm = TM_IN
    const = dict(pipeline_mode=pl.Buffered(1))
    return pl.pallas_call(
        _inproj_kernel,
        grid=(b, s // tm),
        in_specs=[pl.BlockSpec((1, tm, d), lambda bi, i: (bi, i, 0)),
                  pl.BlockSpec((1, N_MOD, d), lambda bi, i: (bi, 0, 0)),
                  pl.BlockSpec((1, d), lambda bi, i: (0, 0)),
                  pl.BlockSpec((d, IN_WIDTH), lambda bi, i: (0, 0), **const)],
        out_specs=[pl.BlockSpec((1, tm // TK, DA_WIDTH, TK), lambda bi, i: (bi, i, 0, 0)),
                   pl.BlockSpec((1, tm, DA_WIDTH), lambda bi, i: (bi, i, 0)),
                   pl.BlockSpec((1, tm // TK, DA_WIDTH, TK), lambda bi, i: (bi, i, 0, 0)),
                   pl.BlockSpec((1, tm, RET_WIDTH), lambda bi, i: (bi, i, 0))],
        out_shape=[jax.ShapeDtypeStruct((b, s // TK, DA_WIDTH, TK), BF16),
                   jax.ShapeDtypeStruct((b, s, DA_WIDTH), BF16),
                   jax.ShapeDtypeStruct((b, s // TK, DA_WIDTH, TK), BF16),
                   jax.ShapeDtypeStruct((b, s, RET_WIDTH), BF16)],
        scratch_shapes=[pltpu.VMEM((RET_HEADS, RET_L, RET_L), F32),
                        pltpu.VMEM((2, RET_HEADS, RET_L, 128), F32),
                        pltpu.VMEM((RET_HEADS, 128, RET_V), F32),
                        pltpu.VMEM((DA_WIDTH, d), BF16),
                        pltpu.VMEM((DA_WIDTH, d), BF16)],
        compiler_params=pltpu.CompilerParams(dimension_semantics=("arbitrary", "arbitrary"),
                                             vmem_limit_bytes=VMEM_LIMIT_BYTES),
        name="inproj",
    )(x, mod, g_pre_mix, w_in_b)


def _dattn_kernel(q_ref, k_ref, vt_ref, lq1_ref, lk1_ref, lq2_ref, lk2_ref, gsub_ref,
                  o_ref, dbias_ref, cpos_ref, acc_ref, *bufs):
    tbufs, pbufs = bufs[:DA_HEADS], bufs[DA_HEADS:]
    qi = pl.program_id(1)
    slopes = [LOG2E * 2.0 ** (-8.0 * (h + 1) / DA_HEADS) for h in range(DA_HEADS)]

    @pl.when((pl.program_id(0) == 0) & (qi == 0))
    def _():
        r = lax.broadcasted_iota(jnp.int32, (TK, 128), 0).astype(F32)
        kl = lax.broadcasted_iota(jnp.int32, (TK, 128), 1)
        c = lax.broadcasted_iota(jnp.int32, (TK, TQ), 0)
        a = lax.broadcasted_iota(jnp.int32, (TK, TQ), 1)
        rel = (a - jnp.abs(a - c)).astype(F32)
        allowed = (c // CHUNK) <= (a // CHUNK)
        for h in range(DA_HEADS):
            pbufs[h][...] = jnp.zeros((TK, 2 * TQ), BF16)
            dbias_ref[h] = jnp.where(allowed, slopes[h] * rel, -jnp.inf)
            x = slopes[h] * r
            hi = x.astype(BF16).astype(F32)
            r1 = x - hi
            mid = r1.astype(BF16).astype(F32)
            lo = r1 - mid
            cpos_ref[h] = jnp.where(kl == 0, hi, jnp.where(kl == 1, mid,
                                    jnp.where(kl == 2, lo, 0.0))).astype(BF16)

    ones_rows = jnp.ones((16, TK), BF16)
    bias_rows = jnp.where(lax.broadcasted_iota(jnp.int32, (128, 2 * TQ), 0) < 3,
                          1.0, 0.0).astype(BF16)
    zq = jnp.zeros((DA_QK, TQ), BF16)
    qqt = []
    for h in range(DA_HEADS):
        qt = q_ref[0, 0, 128 * h:128 * h + 128, :]
        qqt.append(jnp.concatenate([jnp.concatenate([qt[0:DA_QK], zq], axis=1),
                                    jnp.concatenate([zq, qt[DA_QK:]], axis=1),
                                    bias_rows], axis=0))

    def kblock(h, kb):
        return k_ref[0, pl.ds(pl.multiple_of(kb * TK, TK), TK), 128 * h:128 * h + 128]

    def diag_scores(h):
        db = dbias_ref[h]
        s = jnp.dot(kblock(h, qi), qqt[h][0:128], preferred_element_type=F32)
        return s + jnp.concatenate([db, db], axis=1)

    def past_scores(h, kb):
        lhs = jnp.concatenate([kblock(h, kb), cpos_ref[h]], axis=1)
        return jnp.dot(lhs, qqt[h], preferred_element_type=F32)

    def score_stage(h, t):
        tbufs[h][...] = t
        return jnp.max(t, axis=0, keepdims=True)

    def softmax_stage(h, tmax, m, shift):
        m_new = jnp.maximum(m, tmax + shift)
        alpha = jnp.exp2(m - m_new)
        pbufs[h][...] = jnp.exp2(tbufs[h][...] - (m_new - shift)).astype(BF16)
        return alpha, m_new

    def value_stage(h, kb, alpha, l, pending=True):
        lhs = jnp.concatenate([vt_ref[0, kb, 128 * h:128 * h + 128, :], ones_rows], axis=0)
        lhs = jnp.where(pending, lhs, jnp.zeros_like(lhs))
        pv = jnp.dot(lhs, pbufs[h][...], preferred_element_type=F32)
        acc_ref[h] = alpha * acc_ref[h] + pv[0:DA_V]
        return alpha * l + pv[DA_V:DA_V + 1]

    heads = range(DA_HEADS)
    for h in heads:
        acc_ref[h] = jnp.zeros((DA_V, 2 * TQ), F32)
    ms = [jnp.full((1, 2 * TQ), -jnp.inf, F32) for _ in heads]
    ls = [jnp.zeros((1, 2 * TQ), F32) for _ in heads]
    alphas = [jnp.ones((1, 2 * TQ), F32) for _ in heads]
    tmaxs = [score_stage(h, diag_scores(h)) for h in heads]

    def step(n, carry):
        ms, ls, alphas, tmaxs = (list(carry[4 * j:4 * j + 4]) for j in range(4))
        kb_value = jnp.where(n == 1, qi, jnp.maximum(n - 2, 0))
        kb_score = jnp.minimum(n, jnp.maximum(qi - 1, 0))
        rel = jnp.where(n == 0, 0, (n - 1 - qi) * TK).astype(F32)
        for h in heads:
            ls[h] = value_stage(h, kb_value, alphas[h], ls[h], pending=n > 0)
        for h in heads:
            alphas[h], ms[h] = softmax_stage(h, tmaxs[h], ms[h], slopes[h] * rel)
            tmaxs[h] = score_stage(h, past_scores(h, kb_score))
        return tuple(ms) + tuple(ls) + tuple(alphas) + tuple(tmaxs)

    carry = lax.fori_loop(0, qi + 1, step, tuple(ms) + tuple(ls) + tuple(alphas) + tuple(tmaxs))
    ms, ls, alphas = (list(carry[4 * j:4 * j + 4]) for j in range(3))
    kb_last = jnp.maximum(qi - 1, 0)
    for h in heads:
        ls[h] = value_stage(h, kb_last, alphas[h], ls[h])
    carry = [x for h in heads for x in (ms[h], ls[h])]

    lam = (jnp.exp(jnp.sum(lq1_ref[...] * lk1_ref[...], axis=-1, keepdims=True))
           - jnp.exp(jnp.sum(lq2_ref[...] * lk2_ref[...], axis=-1, keepdims=True))
           + LAMBDA_INIT)
    for h in range(DA_HEADS):
        inv = 1.0 / carry[2 * h + 1]
        acc = acc_ref[h]
        ot = acc[:, :TQ] * inv[:, :TQ] - lam * (acc[:, TQ:] * inv[:, TQ:])
        ot = ot * lax.rsqrt(jnp.mean(ot * ot, axis=0, keepdims=True) + RMS_EPS)
        o = ot.T * (gsub_ref[...] * (1.0 - LAMBDA_INIT))
        o_ref[0, :, 128 * h:128 * h + 128] = o.astype(BF16)


def _dattn_call(qt, k, vt, lam_q1, lam_k1, lam_q2, lam_k2, g_sub):
    b, s, _ = k.shape
    nkv = s // TK
    vec = lambda n: pl.BlockSpec((1, n), lambda bi, i: (0, 0))
    return pl.pallas_call(
        _dattn_kernel,
        grid=(b, s // TQ),
        in_specs=[pl.BlockSpec((1, 1, DA_WIDTH, TQ), lambda bi, i: (bi, i, 0, 0)),
                  pl.BlockSpec((1, s, DA_WIDTH), lambda bi, i: (bi, 0, 0)),
                  pl.BlockSpec((1, nkv, DA_WIDTH, TK), lambda bi, i: (bi, 0, 0, 0)),
                  vec(DA_QK), vec(DA_QK), vec(DA_QK), vec(DA_QK), vec(DA_V)],
        out_specs=pl.BlockSpec((1, TQ, DA_WIDTH), lambda bi, i: (bi, i, 0)),
        out_shape=jax.ShapeDtypeStruct((b, s, DA_WIDTH), BF16),
        scratch_shapes=[pltpu.VMEM((DA_HEADS, TK, TQ), F32),
                        pltpu.VMEM((DA_HEADS, TK, 128), BF16),
                        pltpu.VMEM((DA_HEADS, DA_V, 2 * TQ), F32)]
                       + [pltpu.VMEM((TK, 2 * TQ), F32) for _ in range(DA_HEADS)]
                       + [pltpu.VMEM((TK, 2 * TQ), BF16) for _ in range(DA_HEADS)],
        compiler_params=pltpu.CompilerParams(
            dimension_semantics=("arbitrary", "arbitrary"),
            vmem_limit_bytes=VMEM_LIMIT_BYTES),
        name="dattn",
    )(qt, k, vt, lam_q1, lam_k1, lam_q2, lam_k2, g_sub)


def _ffn_kernel(oda_ref, oret_ref, x_ref, mod_ref, gpm_ref, gpf_ref, gpo_ref,
                wout_ref, wup_ref, cw_ref, cb_ref, wdn_ref, o_ref, perm_ref, f_ref, carry_ref):
    i = pl.program_id(1)
    n_sub = len(FFN_SUBS)
    starts = [sum(FFN_SUBS[:t]) for t in range(n_sub)]
    assert sum(FFN_SUBS) == x_ref.shape[1]

    def sub_rows(t):
        return slice(starts[t], starts[t] + FFN_SUBS[t])

    @pl.when(i == 0)
    def _():
        carry_ref[...] = jnp.zeros_like(carry_ref)

    def permute_rows(a, slot):
        n = a.shape[1] // 128
        for s in range(n):
            perm_ref[slot, s, 0:a.shape[0], :] = a[:, 128 * s:128 * s + 128]
        rows = []
        for g in range(a.shape[0] // 64):
            for r in range(8):
                rows.append(jnp.concatenate(
                    [perm_ref[slot, s, pl.ds(64 * g + r, 8, stride=8), :] for s in range(n)],
                    axis=1))
        return jnp.concatenate(rows, axis=0)

    gt1 = mod_ref[0, 2:3, :]
    sh2 = mod_ref[0, 3:4, :]
    sc2 = mod_ref[0, 4:5, :]
    gt2 = mod_ref[0, 5:6, :]
    sub0 = lax.broadcasted_iota(jnp.int32, (8, FF_CW), 0) == 0
    n_chunks = D_FF // FF_CW

    def pre(t):
        rows = sub_rows(t)
        mix = (jnp.dot(oda_ref[0, rows, :], wout_ref[0:DA_WIDTH, :], preferred_element_type=F32)
               + jnp.dot(oret_ref[0, rows, :], wout_ref[DA_WIDTH:, :],
                         preferred_element_type=F32))
        x1 = x_ref[0, rows, :] + _rms(mix) * (gt1 * gpm_ref[...])
        o_ref[0, rows, :] = x1
        return permute_rows(_rms(x1) * (gpf_ref[...] * (1.0 + sc2)) + sh2, t % 2).astype(BF16)

    def conv(y, c0):
        cols = slice(c0, c0 + FF_CW)
        w0, w1, w2, cb = cw_ref[0:1, cols], cw_ref[1:2, cols], cw_ref[2:3, cols], cb_ref[:, cols]
        prev7 = carry_ref[0, :, cols]
        prev6 = carry_ref[1, :, cols]
        out = []
        for g in range(y.shape[0] // 64):
            yg = y[64 * g:64 * g + 64]
            rot7 = pltpu.roll(yg[56:64], 1, 0)
            rot6 = pltpu.roll(yg[48:56], 1, 0)
            sp1 = jnp.where(sub0, prev7, rot7)
            sp2 = jnp.where(sub0, prev6, rot6)
            y1 = jnp.concatenate([sp1, yg[0:56]], axis=0)
            y2 = jnp.concatenate([sp2, sp1, yg[0:48]], axis=0)
            out.append(cb + w0 * y2 + w1 * y1 + w2 * yg)
            prev7, prev6 = rot7, rot6
        carry_ref[0, :, cols] = prev7
        carry_ref[1, :, cols] = prev6
        return jnp.concatenate(out, axis=0)

    def mlp(t, h):
        def up(c):
            return [jnp.dot(h, wup_ref[:, c0:c0 + FF_CW], preferred_element_type=F32)
                    for c0 in (c * FF_CW, D_FF + c * FF_CW)]
        ys = up(0)
        for c in range(n_chunks):
            nxt = up(c + 1) if c + 1 < n_chunks else None
            ug = conv(ys[0], c * FF_CW)
            uv = conv(ys[1], D_FF + c * FF_CW)
            f_ref[sub_rows(t), c * FF_CW:(c + 1) * FF_CW] = (
                ug * jax.nn.sigmoid(ug) * uv).astype(BF16)
            ys = nxt
        return jnp.dot(f_ref[sub_rows(t), :], wdn_ref[...], preferred_element_type=F32)

    def post(t, acc):
        rows = sub_rows(t)
        o_ref[0, rows, :] = (o_ref[0, rows, :]
                             + permute_rows(_rms(acc), 2 + t % 2) * (gt2 * gpo_ref[...]))

    hs = [pre(0)]
    for t in range(n_sub):
        if t + 1 < n_sub:
            hs.append(pre(t + 1))
        acc = mlp(t, hs[t])
        post(t, acc)


def _ffn_call(o_da, o_ret, x, mod, g_post_mix, g_pre_ffn, g_post_ffn, w_out_b, w_up_b,
              conv_w, conv_b, w_down_b):
    b, s, d = x.shape
    tm = TM_FFN
    const = dict(pipeline_mode=pl.Buffered(1))
    vec = lambda n: pl.BlockSpec((1, n), lambda bi, i: (0, 0))
    return pl.pallas_call(
        _ffn_kernel,
        grid=(b, s // tm),
        in_specs=[pl.BlockSpec((1, tm, DA_WIDTH), lambda bi, i: (bi, i, 0)),
                  pl.BlockSpec((1, tm, RET_WIDTH), lambda bi, i: (bi, i, 0)),
                  pl.BlockSpec((1, tm, d), lambda bi, i: (bi, i, 0)),
                  pl.BlockSpec((1, N_MOD, d), lambda bi, i: (bi, 0, 0)),
                  vec(d), vec(d), vec(d),
                  pl.BlockSpec((d, d), lambda bi, i: (0, 0), **const),
                  pl.BlockSpec((d, 2 * D_FF), lambda bi, i: (0, 0), **const),
                  pl.BlockSpec((3, 2 * D_FF), lambda bi, i: (0, 0)),
                  pl.BlockSpec((1, 2 * D_FF), lambda bi, i: (0, 0)),
                  pl.BlockSpec((D_FF, d), lambda bi, i: (0, 0), **const)],
        out_specs=pl.BlockSpec((1, tm, d), lambda bi, i: (bi, i, 0)),
        out_shape=jax.ShapeDtypeStruct((b, s, d), F32),
        scratch_shapes=[pltpu.VMEM((4, d // 128, max(FFN_SUBS), 128), F32),
                        pltpu.VMEM((tm, D_FF), BF16),
                        pltpu.VMEM((2, 8, 2 * D_FF), F32)],
        compiler_params=pltpu.CompilerParams(dimension_semantics=("arbitrary", "arbitrary"),
                                             vmem_limit_bytes=VMEM_LIMIT_BYTES),
        name="ffn",
    )(o_da, o_ret, x, mod, g_post_mix, g_pre_ffn, g_post_ffn, w_out_b, w_up_b,
      conv_w, conv_b, w_down_b)


def kernel(x, c, w_ada, b_ada, g_pre_mix, w_in, lam_q1, lam_k1, lam_q2, lam_k2, g_da_subln,
           w_out, g_post_mix, g_pre_ffn, w_up, conv_w, conv_b, w_down, g_post_ffn):
    b, s, d = x.shape
    depth = w_ada.shape[0]
    assert depth == 1 and d == D_MODEL and s % TM_IN == 0 and s % TQ == 0
    for l in range(depth):
        mod = _mod_call(c, w_ada[l], b_ada[l]).reshape(b, N_MOD, d)
        w_in_b = w_in[l].astype(BF16)
        qt, k, vt, o_ret = _inproj_call(x, mod, g_pre_mix[l][None], w_in_b)
        o_da = _dattn_call(qt, k, vt, lam_q1[l][None], lam_k1[l][None], lam_q2[l][None],
                           lam_k2[l][None], g_da_subln[l][None])
        x = _ffn_call(o_da, o_ret, x, mod, g_post_mix[l][None], g_pre_ffn[l][None],
                      g_post_ffn[l][None], w_out[l].astype(BF16), w_up[l].astype(BF16),
                      conv_w[l], conv_b[l][None], w_down[l].astype(BF16))
    return x
```

```python
import math

import jax
import jax.numpy as jnp
from jax import lax
from jax.experimental import pallas as pl
from jax.experimental.pallas import tpu as pltpu

D_MODEL = 1024
CHUNK = 64
DA_HEADS = 4
DA_QK = 64
DA_V = 128
DA_WIDTH = 512
RET_HEADS = 4
RET_QK = 64
RET_V = 128
RET_WIDTH = 512
IN_WIDTH = 3072
COL_DA_Q = 0
COL_DA_K = COL_DA_Q + DA_HEADS * 2 * DA_QK
COL_DA_V = COL_DA_K + DA_HEADS * 2 * DA_QK
COL_RET_Q = COL_DA_V + DA_HEADS * DA_V
COL_RET_K = COL_RET_Q + RET_HEADS * RET_QK
COL_RET_V = COL_RET_K + RET_HEADS * RET_QK
COL_RET_G = COL_RET_V + RET_HEADS * RET_V
assert COL_RET_G + RET_HEADS * RET_V == IN_WIDTH
D_FF = 2816
N_MOD = 6
RMS_EPS = 1e-6
LAMBDA_INIT = 0.8 - 0.6 * math.exp(-0.3 * 0)

LOG2E = 1.4426950408889634
Q_FOLD = (DA_QK ** -0.5) * LOG2E
RET_K_FOLD = RET_QK ** -0.5

VMEM_LIMIT_BYTES = 56 * 1024 * 1024

BF16 = jnp.bfloat16
F32 = jnp.float32

LANES = 128
SUBLANES = 8
PERM_GROUP = SUBLANES * SUBLANES

TM_IN = 1024
TQ = 512
TK = 512
RET_L = 256
TM_FFN = 512
FFN_SUBS = (192, 320)
FF_CW = 256
assert all(n % PERM_GROUP == 0 for n in FFN_SUBS) and sum(FFN_SUBS) == TM_FFN


def _rms(x):
    return x * lax.rsqrt(jnp.mean(x * x, axis=-1, keepdims=True) + RMS_EPS)


def _mod_kernel(c_ref, w_ref, b_ref, o_ref):
    c = c_ref[...]
    a = c * jax.nn.sigmoid(c)
    o_ref[...] = jnp.dot(a.astype(BF16), w_ref[...].astype(BF16),
                         preferred_element_type=F32) + b_ref[...]


def _mod_call(c, w_ada, b_ada):
    b, d = c.shape
    n = w_ada.shape[1]
    tn = 1024
    return pl.pallas_call(
        _mod_kernel,
        grid=(n // tn,),
        in_specs=[pl.BlockSpec((b, d), lambda j: (0, 0)),
                  pl.BlockSpec((d, tn), lambda j: (0, j)),
                  pl.BlockSpec((1, tn), lambda j: (0, j))],
        out_specs=pl.BlockSpec((b, tn), lambda j: (0, j)),
        out_shape=jax.ShapeDtypeStruct((b, n), F32),
        compiler_params=pltpu.CompilerParams(dimension_semantics=("arbitrary",),
                                             vmem_limit_bytes=VMEM_LIMIT_BYTES),
        name="mod",
    )(c, w_ada, b_ada.reshape(1, n))


def _inproj_kernel(x_ref, mod_ref, g_ref, w_ref, qt_ref, k_ref, vt_ref, oret_ref,
                   dec_ref, rowdec_ref, state_ref, wqt_ref, wvt_ref):
    i = pl.program_id(1)
    tm = x_ref.shape[1]
    L = RET_L
    lgs = [math.log(1.0 - 2.0 ** (-5.0 - h)) for h in range(RET_HEADS)]

    @pl.when(i == 0)
    def _():
        state_ref[...] = jnp.zeros_like(state_ref)

    @pl.when((pl.program_id(0) == 0) & (i == 0))
    def _():
        row = lax.broadcasted_iota(jnp.int32, (L, L), 0)
        col = lax.broadcasted_iota(jnp.int32, (L, L), 1)
        d = row - col
        same = (row // CHUNK) == (col // CHUNK)
        past = (col // CHUNK) < (row // CHUNK)
        e = jnp.where(same, jnp.abs(d), d).astype(F32)
        r = lax.broadcasted_iota(jnp.int32, (L, 128), 0).astype(F32)
        for h in range(RET_HEADS):
            dec_ref[h] = jnp.where(same | past, jnp.exp(lgs[h] * e), 0.0)
            rowdec_ref[0, h] = jnp.exp(lgs[h] * (r + 1.0))
            rowdec_ref[1, h] = jnp.exp(lgs[h] * (L - 1.0 - r))
        wqt_ref[...] = w_ref[:, COL_DA_Q:COL_DA_K].astype(F32).T.astype(BF16)
        wvt_ref[...] = w_ref[:, COL_DA_V:COL_RET_Q].astype(F32).T.astype(BF16)

    x = x_ref[0]
    sh = mod_ref[0, 0:1, :]
    sc = mod_ref[0, 1:2, :]
    hb = (_rms(x) * (g_ref[...] * (1.0 + sc)) + sh).astype(BF16)

    def proj(lo, hi):
        return jnp.dot(hb, w_ref[:, lo:hi], preferred_element_type=F32)

    rq = proj(COL_RET_Q, COL_RET_K).astype(BF16)
    rk = (proj(COL_RET_K, COL_RET_V) * RET_K_FOLD).astype(BF16)
    rv = proj(COL_RET_V, COL_RET_G).astype(BF16)
    rg = proj(COL_RET_G, IN_WIDTH)

    lane = lax.broadcasted_iota(jnp.int32, (L, 128), 1)
    nt = (((1,), (1,)), ((), ()))
    blocks = [(t, h) for t in range(tm // L) for h in range(RET_HEADS)]
    qz, kz, vh, sb, kd = {}, {}, {}, {}, {}
    for t, h in blocks:
        rows = slice(t * L, (t + 1) * L)
        pair = slice(128 * (h // 2), 128 * (h // 2) + 128)
        lo = RET_QK * (h % 2)
        sel = (lane >= lo) & (lane < lo + RET_QK)
        qp, kp = rq[rows, pair], rk[rows, pair]
        qz[t, h] = jnp.where(sel, qp, jnp.zeros_like(qp))
        kz[t, h] = jnp.where(sel, kp, jnp.zeros_like(kp))
        vh[t, h] = rv[rows, 128 * h:128 * h + 128]
        s = lax.dot_general(qz[t, h], kz[t, h], nt, preferred_element_type=F32)
        sb[t, h] = (s * dec_ref[h]).astype(BF16)
        kd[t, h] = (kz[t, h].astype(F32) * rowdec_ref[1, h]).astype(BF16)

    def proj_t(wt_ref, out_ref, fold):
        t = lax.dot_general(wt_ref[...], hb, nt, preferred_element_type=F32)
        t = (t if fold is None else t * fold).astype(BF16)
        for j in range(out_ref.shape[1]):
            out_ref[0, j] = t[:, j * TK:(j + 1) * TK]

    proj_t(wqt_ref, qt_ref, Q_FOLD)

    def retention_block(t):
        rows = slice(t * L, (t + 1) * L)
        for h in range(RET_HEADS):
            st = state_ref[h]
            o = jnp.dot(sb[t, h], vh[t, h], preferred_element_type=F32)
            o = o + rowdec_ref[0, h] * jnp.dot(qz[t, h], st.astype(BF16),
                                               preferred_element_type=F32)
            u = lax.dot_general(kd[t, h], vh[t, h], (((0,), (0,)), ((), ())),
                                preferred_element_type=F32)
            state_ref[h] = math.exp(lgs[h] * L) * st + u
            g = rg[rows, 128 * h:128 * h + 128]
            oret_ref[0, rows, 128 * h:128 * h + 128] = (
                _rms(o) * (g * jax.nn.sigmoid(g))).astype(BF16)

    retention_block(0)
    k_ref[0] = proj(COL_DA_K, COL_DA_V).astype(BF16)
    for t in range(1, tm // L):
        retention_block(t)
    proj_t(wvt_ref, vt_ref, None)


def _inproj_call(x, mod, g_pre_mix, w_in_b):
    b, s, d = x.shape
    tm = TM_IN
    const = dict(pipeline_mode=pl.Buffered(1))
    return pl.pallas_call(
        _inproj_kernel,
        grid=(b, s // tm),
        in_specs=[pl.BlockSpec((1, tm, d), lambda bi, i: (bi, i, 0)),
                  pl.BlockSpec((1, N_MOD, d), lambda bi, i: (bi, 0, 0)),
                  pl.BlockSpec((1, d), lambda bi, i: (0, 0)),
                  pl.BlockSpec((d, IN_WIDTH), lambda bi, i: (0, 0), **const)],
        out_specs=[pl.BlockSpec((1, tm // TK, DA_WIDTH, TK), lambda bi, i: (bi, i, 0, 0)),
                   pl.BlockSpec((1, tm, DA_WIDTH), lambda bi, i: (bi, i, 0)),
                   pl.BlockSpec((1, tm // TK, DA_WIDTH, TK), lambda bi, i: (bi, i, 0, 0)),
                   pl.BlockSpec((1, tm, RET_WIDTH), lambda bi, i: (bi, i, 0))],
        out_shape=[jax.ShapeDtypeStruct((b, s // TK, DA_WIDTH, TK), BF16),
                   jax.ShapeDtypeStruct((b, s, DA_WIDTH), BF16),
                   jax.ShapeDtypeStruct((b, s // TK, DA_WIDTH, TK), BF16),
                   jax.ShapeDtypeStruct((b, s, RET_WIDTH), BF16)],
        scratch_shapes=[pltpu.VMEM((RET_HEADS, RET_L, RET_L), F32),
                        pltpu.VMEM((2, RET_HEADS, RET_L, 128), F32),
                        pltpu.VMEM((RET_HEADS, 128, RET_V), F32),
                        pltpu.VMEM((DA_WIDTH, d), BF16),
                        pltpu.VMEM((DA_WIDTH, d), BF16)],
        compiler_params=pltpu.CompilerParams(dimension_semantics=("arbitrary", "arbitrary"),
                                             vmem_limit_bytes=VMEM_LIMIT_BYTES),
        name="inproj",
    )(x, mod, g_pre_mix, w_in_b)


def _dattn_kernel(q_ref, k_ref, vt_ref, lq1_ref, lk1_ref, lq2_ref, lk2_ref, gsub_ref,
                  o_ref, dbias_ref, cpos_ref, acc_ref, *bufs):
    tbufs, pbufs = bufs[:DA_HEADS], bufs[DA_HEADS:]
    qi = pl.program_id(1)
    slopes = [LOG2E * 2.0 ** (-8.0 * (h + 1) / DA_HEADS) for h in range(DA_HEADS)]

    @pl.when((pl.program_id(0) == 0) & (qi == 0))
    def _():
        r = lax.broadcasted_iota(jnp.int32, (TK, 128), 0).astype(F32)
        kl = lax.broadcasted_iota(jnp.int32, (TK, 128), 1)
        c = lax.broadcasted_iota(jnp.int32, (TK, TQ), 0)
        a = lax.broadcasted_iota(jnp.int32, (TK, TQ), 1)
        rel = (a - jnp.abs(a - c)).astype(F32)
        allowed = (c // CHUNK) <= (a // CHUNK)
        for h in range(DA_HEADS):
            pbufs[h][...] = jnp.zeros((TK, 2 * TQ), BF16)
            dbias_ref[h] = jnp.where(allowed, slopes[h] * rel, -jnp.inf)
            x = slopes[h] * r
            hi = x.astype(BF16).astype(F32)
            r1 = x - hi
            mid = r1.astype(BF16).astype(F32)
            lo = r1 - mid
            cpos_ref[h] = jnp.where(kl == 0, hi, jnp.where(kl == 1, mid,
                                    jnp.where(kl == 2, lo, 0.0))).astype(BF16)

    ones_rows = jnp.ones((16, TK), BF16)
    bias_rows = jnp.where(lax.broadcasted_iota(jnp.int32, (128, 2 * TQ), 0) < 3,
                          1.0, 0.0).astype(BF16)
    zq = jnp.zeros((DA_QK, TQ), BF16)
    qqt = []
    for h in range(DA_HEADS):
        qt = q_ref[0, 0, 128 * h:128 * h + 128, :]
        qqt.append(jnp.concatenate([jnp.concatenate([qt[0:DA_QK], zq], axis=1),
                                    jnp.concatenate([zq, qt[DA_QK:]], axis=1),
                                    bias_rows], axis=0))

    def kblock(h, kb):
        return k_ref[0, pl.ds(pl.multiple_of(kb * TK, TK), TK), 128 * h:128 * h + 128]

    def diag_scores(h):
        db = dbias_ref[h]
        s = jnp.dot(kblock(h, qi), qqt[h][0:128], preferred_element_type=F32)
        return s + jnp.concatenate([db, db], axis=1)

    def past_scores(h, kb):
        lhs = jnp.concatenate([kblock(h, kb), cpos_ref[h]], axis=1)
        return jnp.dot(lhs, qqt[h], preferred_element_type=F32)

    def score_stage(h, t):
        tbufs[h][...] = t
        return jnp.max(t, axis=0, keepdims=True)

    def softmax_stage(h, tmax, m, shift):
        m_new = jnp.maximum(m, tmax + shift)
        alpha = jnp.exp2(m - m_new)
        pbufs[h][...] = jnp.exp2(tbufs[h][...] - (m_new - shift)).astype(BF16)
        return alpha, m_new

    def value_stage(h, kb, alpha, l, pending=True):
        lhs = jnp.concatenate([vt_ref[0, kb, 128 * h:128 * h + 128, :], ones_rows], axis=0)
        lhs = jnp.where(pending, lhs, jnp.zeros_like(lhs))
        pv = jnp.dot(lhs, pbufs[h][...], preferred_element_type=F32)
        acc_ref[h] = alpha * acc_ref[h] + pv[0:DA_V]
        return alpha * l + pv[DA_V:DA_V + 1]

    heads = range(DA_HEADS)
    for h in heads:
        acc_ref[h] = jnp.zeros((DA_V, 2 * TQ), F32)
    ms = [jnp.full((1, 2 * TQ), -jnp.inf, F32) for _ in heads]
    ls = [jnp.zeros((1, 2 * TQ), F32) for _ in heads]
    alphas = [jnp.ones((1, 2 * TQ), F32) for _ in heads]
    tmaxs = [score_stage(h, diag_scores(h)) for h in heads]

    def step(n, carry):
        ms, ls, alphas, tmaxs = (list(carry[4 * j:4 * j + 4]) for j in range(4))
        kb_value = jnp.where(n == 1, qi, jnp.maximum(n - 2, 0))
        kb_score = jnp.minimum(n, jnp.maximum(qi - 1, 0))
        rel = jnp.where(n == 0, 0, (n - 1 - qi) * TK).astype(F32)
        for h in heads:
            ls[h] = value_stage(h, kb_value, alphas[h], ls[h], pending=n > 0)
        for h in heads:
            alphas[h], ms[h] = softmax_stage(h, tmaxs[h], ms[h], slopes[h] * rel)
            tmaxs[h] = score_stage(h, past_scores(h, kb_score))
        return tuple(ms) + tuple(ls) + tuple(alphas) + tuple(tmaxs)

    carry = lax.fori_loop(0, qi + 1, step, tuple(ms) + tuple(ls) + tuple(alphas) + tuple(tmaxs))
    ms, ls, alphas = (list(carry[4 * j:4 * j + 4]) for j in range(3))
    kb_last = jnp.maximum(qi - 1, 0)
    for h in heads:
        ls[h] = value_stage(h, kb_last, alphas[h], ls[h])
    carry = [x for h in heads for x in (ms[h], ls[h])]

    lam = (jnp.exp(jnp.sum(lq1_ref[...] * lk1_ref[...], axis=-1, keepdims=True))
           - jnp.exp(jnp.sum(lq2_ref[...] * lk2_ref[...], axis=-1, keepdims=True))
           + LAMBDA_INIT)
    for h in range(DA_HEADS):
        inv = 1.0 / carry[2 * h + 1]
        acc = acc_ref[h]
        ot = acc[:, :TQ] * inv[:, :TQ] - lam * (acc[:, TQ:] * inv[:, TQ:])
        ot = ot * lax.rsqrt(jnp.mean(ot * ot, axis=0, keepdims=True) + RMS_EPS)
        o = ot.T * (gsub_ref[...] * (1.0 - LAMBDA_INIT))
        o_ref[0, :, 128 * h:128 * h + 128] = o.astype(BF16)


def _dattn_call(qt, k, vt, lam_q1, lam_k1, lam_q2, lam_k2, g_sub):
    b, s, _ = k.shape
    nkv = s // TK
    vec = lambda n: pl.BlockSpec((1, n), lambda bi, i: (0, 0))
    return pl.pallas_call(
        _dattn_kernel,
        grid=(b, s // TQ),
        in_specs=[pl.BlockSpec((1, 1, DA_WIDTH, TQ), lambda bi, i: (bi, i, 0, 0)),
                  pl.BlockSpec((1, s, DA_WIDTH), lambda bi, i: (bi, 0, 0)),
                  pl.BlockSpec((1, nkv, DA_WIDTH, TK), lambda bi, i: (bi, 0, 0, 0)),
                  vec(DA_QK), vec(DA_QK), vec(DA_QK), vec(DA_QK), vec(DA_V)],
        out_specs=pl.BlockSpec((1, TQ, DA_WIDTH), lambda bi, i: (bi, i, 0)),
        out_shape=jax.ShapeDtypeStruct((b, s, DA_WIDTH), BF16),
        scratch_shapes=[pltpu.VMEM((DA_HEADS, TK, TQ), F32),
                        pltpu.VMEM((DA_HEADS, TK, 128), BF16),
                        pltpu.VMEM((DA_HEADS, DA_V, 2 * TQ), F32)]
                       + [pltpu.VMEM((TK, 2 * TQ), F32) for _ in range(DA_HEADS)]
                       + [pltpu.VMEM((TK, 2 * TQ), BF16) for _ in range(DA_HEADS)],
        compiler_params=pltpu.CompilerParams(
            dimension_semantics=("arbitrary", "arbitrary"),
            vmem_limit_bytes=VMEM_LIMIT_BYTES),
        name="dattn",
    )(qt, k, vt, lam_q1, lam_k1, lam_q2, lam_k2, g_sub)


def _ffn_kernel(oda_ref, oret_ref, x_ref, mod_ref, gpm_ref, gpf_ref, gpo_ref,
                wout_ref, wup_ref, cw_ref, cb_ref, wdn_ref, o_ref, perm_ref, f_ref, carry_ref):
    i = pl.program_id(1)
    n_sub = len(FFN_SUBS)
    starts = [sum(FFN_SUBS[:t]) for t in range(n_sub)]
    assert sum(FFN_SUBS) == x_ref.shape[1]

    def sub_rows(t):
        return slice(starts[t], starts[t] + FFN_SUBS[t])

    @pl.when(i == 0)
    def _():
        carry_ref[...] = jnp.zeros_like(carry_ref)

    def permute_rows(a, slot):
        n = a.shape[1] // LANES
        for s in range(n):
            perm_ref[slot, s, 0:a.shape[0], :] = a[:, LANES * s:LANES * (s + 1)]
        rows = []
        for g in range(a.shape[0] // PERM_GROUP):
            for r in range(SUBLANES):
                rows.append(jnp.concatenate(
                    [perm_ref[slot, s, pl.ds(PERM_GROUP * g + r, SUBLANES, stride=SUBLANES), :]
                     for s in range(n)], axis=1))
        return jnp.concatenate(rows, axis=0)

    gt1 = mod_ref[0, 2:3, :]
    sh2 = mod_ref[0, 3:4, :]
    sc2 = mod_ref[0, 4:5, :]
    gt2 = mod_ref[0, 5:6, :]
    sub0 = lax.broadcasted_iota(jnp.int32, (SUBLANES, FF_CW), 0) == 0
    n_chunks = D_FF // FF_CW

    def pre(t):
        rows = sub_rows(t)
        mix = (jnp.dot(oda_ref[0, rows, :], wout_ref[0:DA_WIDTH, :], preferred_element_type=F32)
               + jnp.dot(oret_ref[0, rows, :], wout_ref[DA_WIDTH:, :],
                         preferred_element_type=F32))
        x1 = x_ref[0, rows, :] + _rms(mix) * (gt1 * gpm_ref[...])
        o_ref[0, rows, :] = x1
        return permute_rows(_rms(x1) * (gpf_ref[...] * (1.0 + sc2)) + sh2, t % 2).astype(BF16)

    def conv(y, c0):
        cols = slice(c0, c0 + FF_CW)
        w0, w1, w2, cb = cw_ref[0:1, cols], cw_ref[1:2, cols], cw_ref[2:3, cols], cb_ref[:, cols]
        prev7 = carry_ref[0, :, cols]
        prev6 = carry_ref[1, :, cols]
        out = []
        g1, g2 = PERM_GROUP - SUBLANES, PERM_GROUP - 2 * SUBLANES
        for g in range(y.shape[0] // PERM_GROUP):
            yg = y[PERM_GROUP * g:PERM_GROUP * (g + 1)]
            rot7 = pltpu.roll(yg[g1:PERM_GROUP], 1, 0)
            rot6 = pltpu.roll(yg[g2:g1], 1, 0)
            sp1 = jnp.where(sub0, prev7, rot7)
            sp2 = jnp.where(sub0, prev6, rot6)
            y1 = jnp.concatenate([sp1, yg[0:g1]], axis=0)
            y2 = jnp.concatenate([sp2, sp1, yg[0:g2]], axis=0)
            out.append(cb + w0 * y2 + w1 * y1 + w2 * yg)
            prev7, prev6 = rot7, rot6
        carry_ref[0, :, cols] = prev7
        carry_ref[1, :, cols] = prev6
        return jnp.concatenate(out, axis=0)

    def mlp(t, h):
        def up(c):
            return [jnp.dot(h, wup_ref[:, c0:c0 + FF_CW], preferred_element_type=F32)
                    for c0 in (c * FF_CW, D_FF + c * FF_CW)]
        ys = up(0)
        for c in range(n_chunks):
            nxt = up(c + 1) if c + 1 < n_chunks else None
            ug = conv(ys[0], c * FF_CW)
            uv = conv(ys[1], D_FF + c * FF_CW)
            f_ref[sub_rows(t), c * FF_CW:(c + 1) * FF_CW] = (
                ug * jax.nn.sigmoid(ug) * uv).astype(BF16)
            ys = nxt
        return jnp.dot(f_ref[sub_rows(t), :], wdn_ref[...], preferred_element_type=F32)

    def post(t, acc):
        rows = sub_rows(t)
        o_ref[0, rows, :] = (o_ref[0, rows, :]
                             + permute_rows(_rms(acc), 2 + t % 2) * (gt2 * gpo_ref[...]))

    hs = [pre(0)]
    for t in range(n_sub):
        if t + 1 < n_sub:
            hs.append(pre(t + 1))
        acc = mlp(t, hs[t])
        post(t, acc)


def _ffn_call(o_da, o_ret, x, mod, g_post_mix, g_pre_ffn, g_post_ffn, w_out_b, w_up_b,
              conv_w, conv_b, w_down_b):
    b, s, d = x.shape
    tm = TM_FFN
    const = dict(pipeline_mode=pl.Buffered(1))
    vec = lambda n: pl.BlockSpec((1, n), lambda bi, i: (0, 0))
    return pl.pallas_call(
        _ffn_kernel,
        grid=(b, s // tm),
        in_specs=[pl.BlockSpec((1, tm, DA_WIDTH), lambda bi, i: (bi, i, 0)),
                  pl.BlockSpec((1, tm, RET_WIDTH), lambda bi, i: (bi, i, 0)),
                  pl.BlockSpec((1, tm, d), lambda bi, i: (bi, i, 0)),
                  pl.BlockSpec((1, N_MOD, d), lambda bi, i: (bi, 0, 0)),
                  vec(d), vec(d), vec(d),
                  pl.BlockSpec((d, d), lambda bi, i: (0, 0), **const),
                  pl.BlockSpec((d, 2 * D_FF), lambda bi, i: (0, 0), **const),
                  pl.BlockSpec((3, 2 * D_FF), lambda bi, i: (0, 0)),
                  pl.BlockSpec((1, 2 * D_FF), lambda bi, i: (0, 0)),
                  pl.BlockSpec((D_FF, d), lambda bi, i: (0, 0), **const)],
        out_specs=pl.BlockSpec((1, tm, d), lambda bi, i: (bi, i, 0)),
        out_shape=jax.ShapeDtypeStruct((b, s, d), F32),
        scratch_shapes=[pltpu.VMEM((4, d // LANES, max(FFN_SUBS), LANES), F32),
                        pltpu.VMEM((tm, D_FF), BF16),
                        pltpu.VMEM((2, SUBLANES, 2 * D_FF), F32)],
        compiler_params=pltpu.CompilerParams(dimension_semantics=("arbitrary", "arbitrary"),
                                             vmem_limit_bytes=VMEM_LIMIT_BYTES),
        name="ffn",
    )(o_da, o_ret, x, mod, g_post_mix, g_pre_ffn, g_post_ffn, w_out_b, w_up_b,
      conv_w, conv_b, w_down_b)


def kernel(x, c, w_ada, b_ada, g_pre_mix, w_in, lam_q1, lam_k1, lam_q2, lam_k2, g_da_subln,
           w_out, g_post_mix, g_pre_ffn, w_up, conv_w, conv_b, w_down, g_post_ffn):
    b, s, d = x.shape
    depth = w_ada.shape[0]
    assert depth == 1 and d == D_MODEL and s % TM_IN == 0 and s % TQ == 0
    for l in range(depth):
        mod = _mod_call(c, w_ada[l], b_ada[l]).reshape(b, N_MOD, d)
        w_in_b = w_in[l].astype(BF16)
        qt, k, vt, o_ret = _inproj_call(x, mod, g_pre_mix[l][None], w_in_b)
        o_da = _dattn_call(qt, k, vt, lam_q1[l][None], lam_k1[l][None], lam_q2[l][None],
                           lam_k2[l][None], g_da_subln[l][None])
        x = _ffn_call(o_da, o_ret, x, mod, g_post_mix[l][None], g_pre_ffn[l][None],
                      g_post_ffn[l][None], w_out[l].astype(BF16), w_up[l].astype(BF16),
                      conv_w[l], conv_b[l][None], w_down[l].astype(BF16))
    return x
```

```python
import math

import jax
import jax.numpy as jnp
from jax import lax
from jax.experimental import pallas as pl
from jax.experimental.pallas import tpu as pltpu

D_MODEL = 1024
CHUNK = 64
DA_HEADS = 4
DA_QK = 64
DA_V = 128
DA_WIDTH = 512
RET_HEADS = 4
RET_QK = 64
RET_V = 128
RET_WIDTH = 512
IN_WIDTH = 3072
COL_DA_Q = 0
COL_DA_K = COL_DA_Q + DA_HEADS * 2 * DA_QK
COL_DA_V = COL_DA_K + DA_HEADS * 2 * DA_QK
COL_RET_Q = COL_DA_V + DA_HEADS * DA_V
COL_RET_K = COL_RET_Q + RET_HEADS * RET_QK
COL_RET_V = COL_RET_K + RET_HEADS * RET_QK
COL_RET_G = COL_RET_V + RET_HEADS * RET_V
assert COL_RET_G + RET_HEADS * RET_V == IN_WIDTH
D_FF = 2816
N_MOD = 6
RMS_EPS = 1e-6
LAMBDA_INIT = 0.8 - 0.6 * math.exp(-0.3 * 0)

LOG2E = 1.4426950408889634
Q_FOLD = (DA_QK ** -0.5) * LOG2E
RET_K_FOLD = RET_QK ** -0.5

VMEM_LIMIT_BYTES = 56 * 1024 * 1024

BF16 = jnp.bfloat16
F32 = jnp.float32

LANES = 128
SUBLANES = 8
PERM_GROUP = SUBLANES * SUBLANES

TM_IN = 1024
TQ = 512
TK = 512
RET_L = 256
TM_FFN = 1024
FFN_SUBS = (192, 320, 256, 256)
FF_CW = 256
assert all(n % PERM_GROUP == 0 for n in FFN_SUBS) and sum(FFN_SUBS) == TM_FFN


def _rms(x):
    return x * lax.rsqrt(jnp.mean(x * x, axis=-1, keepdims=True) + RMS_EPS)


def _mod_kernel(c_ref, w_ref, b_ref, o_ref):
    c = c_ref[...]
    a = c * jax.nn.sigmoid(c)
    o_ref[...] = jnp.dot(a.astype(BF16), w_ref[...].astype(BF16),
                         preferred_element_type=F32) + b_ref[...]


def _mod_call(c, w_ada, b_ada):
    b, d = c.shape
    n = w_ada.shape[1]
    tn = 1024
    return pl.pallas_call(
        _mod_kernel,
        grid=(n // tn,),
        in_specs=[pl.BlockSpec((b, d), lambda j: (0, 0)),
                  pl.BlockSpec((d, tn), lambda j: (0, j)),
                  pl.BlockSpec((1, tn), lambda j: (0, j))],
        out_specs=pl.BlockSpec((b, tn), lambda j: (0, j)),
        out_shape=jax.ShapeDtypeStruct((b, n), F32),
        compiler_params=pltpu.CompilerParams(dimension_semantics=("arbitrary",),
                                             vmem_limit_bytes=VMEM_LIMIT_BYTES),
        name="mod",
    )(c, w_ada, b_ada.reshape(1, n))


def _inproj_kernel(x_ref, mod_ref, g_ref, w_ref, qt_ref, k_ref, vt_ref, oret_ref,
                   dec_ref, rowdec_ref, state_ref, wqt_ref, wvt_ref):
    i = pl.program_id(1)
    tm = x_ref.shape[1]
    L = RET_L
    lgs = [math.log(1.0 - 2.0 ** (-5.0 - h)) for h in range(RET_HEADS)]

    @pl.when(i == 0)
    def _():
        state_ref[...] = jnp.zeros_like(state_ref)

    @pl.when((pl.program_id(0) == 0) & (i == 0))
    def _():
        row = lax.broadcasted_iota(jnp.int32, (L, L), 0)
        col = lax.broadcasted_iota(jnp.int32, (L, L), 1)
        d = row - col
        same = (row // CHUNK) == (col // CHUNK)
        past = (col // CHUNK) < (row // CHUNK)
        e = jnp.where(same, jnp.abs(d), d).astype(F32)
        r = lax.broadcasted_iota(jnp.int32, (L, 128), 0).astype(F32)
        for h in range(RET_HEADS):
            dec_ref[h] = jnp.where(same | past, jnp.exp(lgs[h] * e), 0.0)
            rowdec_ref[0, h] = jnp.exp(lgs[h] * (r + 1.0))
            rowdec_ref[1, h] = jnp.exp(lgs[h] * (L - 1.0 - r))
        wqt_ref[...] = w_ref[:, COL_DA_Q:COL_DA_K].astype(F32).T.astype(BF16)
        wvt_ref[...] = w_ref[:, COL_DA_V:COL_RET_Q].astype(F32).T.astype(BF16)

    x = x_ref[0]
    sh = mod_ref[0, 0:1, :]
    sc = mod_ref[0, 1:2, :]
    hb = (_rms(x) * (g_ref[...] * (1.0 + sc)) + sh).astype(BF16)

    def proj(lo, hi):
        return jnp.dot(hb, w_ref[:, lo:hi], preferred_element_type=F32)

    rq = proj(COL_RET_Q, COL_RET_K).astype(BF16)
    rk = (proj(COL_RET_K, COL_RET_V) * RET_K_FOLD).astype(BF16)
    rv = proj(COL_RET_V, COL_RET_G).astype(BF16)
    rg = proj(COL_RET_G, IN_WIDTH)

    lane = lax.broadcasted_iota(jnp.int32, (L, 128), 1)
    nt = (((1,), (1,)), ((), ()))
    blocks = [(t, h) for t in range(tm // L) for h in range(RET_HEADS)]
    qz, kz, vh, sb, kd = {}, {}, {}, {}, {}
    for t, h in blocks:
        rows = slice(t * L, (t + 1) * L)
        pair = slice(128 * (h // 2), 128 * (h // 2) + 128)
        lo = RET_QK * (h % 2)
        sel = (lane >= lo) & (lane < lo + RET_QK)
        qp, kp = rq[rows, pair], rk[rows, pair]
        qz[t, h] = jnp.where(sel, qp, jnp.zeros_like(qp))
        kz[t, h] = jnp.where(sel, kp, jnp.zeros_like(kp))
        vh[t, h] = rv[rows, 128 * h:128 * h + 128]
        s = lax.dot_general(qz[t, h], kz[t, h], nt, preferred_element_type=F32)
        sb[t, h] = (s * dec_ref[h]).astype(BF16)
        kd[t, h] = (kz[t, h].astype(F32) * rowdec_ref[1, h]).astype(BF16)

    def proj_t(wt_ref, out_ref, fold):
        t = lax.dot_general(wt_ref[...], hb, nt, preferred_element_type=F32)
        t = (t if fold is None else t * fold).astype(BF16)
        for j in range(out_ref.shape[1]):
            out_ref[0, j] = t[:, j * TK:(j + 1) * TK]

    proj_t(wqt_ref, qt_ref, Q_FOLD)

    def retention_block(t):
        rows = slice(t * L, (t + 1) * L)
        for h in range(RET_HEADS):
            st = state_ref[h]
            o = jnp.dot(sb[t, h], vh[t, h], preferred_element_type=F32)
            o = o + rowdec_ref[0, h] * jnp.dot(qz[t, h], st.astype(BF16),
                                               preferred_element_type=F32)
            u = lax.dot_general(kd[t, h], vh[t, h], (((0,), (0,)), ((), ())),
                                preferred_element_type=F32)
            state_ref[h] = math.exp(lgs[h] * L) * st + u
            g = rg[rows, 128 * h:128 * h + 128]
            oret_ref[0, rows, 128 * h:128 * h + 128] = (
                _rms(o) * (g * jax.nn.sigmoid(g))).astype(BF16)

    retention_block(0)
    k_ref[0] = proj(COL_DA_K, COL_DA_V).astype(BF16)
    for t in range(1, tm // L):
        retention_block(t)
    proj_t(wvt_ref, vt_ref, None)


def _inproj_call(x, mod, g_pre_mix, w_in_b):
    b, s, d = x.shape
    tm = TM_IN
    const = dict(pipeline_mode=pl.Buffered(1))
    return pl.pallas_call(
        _inproj_kernel,
        grid=(b, s // tm),
        in_specs=[pl.BlockSpec((1, tm, d), lambda bi, i: (bi, i, 0)),
                  pl.BlockSpec((1, N_MOD, d), lambda bi, i: (bi, 0, 0)),
                  pl.BlockSpec((1, d), lambda bi, i: (0, 0)),
                  pl.BlockSpec((d, IN_WIDTH), lambda bi, i: (0, 0), **const)],
        out_specs=[pl.BlockSpec((1, tm // TK, DA_WIDTH, TK), lambda bi, i: (bi, i, 0, 0)),
                   pl.BlockSpec((1, tm, DA_WIDTH), lambda bi, i: (bi, i, 0)),
                   pl.BlockSpec((1, tm // TK, DA_WIDTH, TK), lambda bi, i: (bi, i, 0, 0)),
                   pl.BlockSpec((1, tm, RET_WIDTH), lambda bi, i: (bi, i, 0))],
        out_shape=[jax.ShapeDtypeStruct((b, s // TK, DA_WIDTH, TK), BF16),
                   jax.ShapeDtypeStruct((b, s, DA_WIDTH), BF16),
                   jax.ShapeDtypeStruct((b, s // TK, DA_WIDTH, TK), BF16),
                   jax.ShapeDtypeStruct((b, s, RET_WIDTH), BF16)],
        scratch_shapes=[pltpu.VMEM((RET_HEADS, RET_L, RET_L), F32),
                        pltpu.VMEM((2, RET_HEADS, RET_L, 128), F32),
                        pltpu.VMEM((RET_HEADS, 128, RET_V), F32),
                        pltpu.VMEM((DA_WIDTH, d), BF16),
                        pltpu.VMEM((DA_WIDTH, d), BF16)],
        compiler_params=pltpu.CompilerParams(dimension_semantics=("arbitrary", "arbitrary"),
                                             vmem_limit_bytes=VMEM_LIMIT_BYTES),
        name="inproj",
    )(x, mod, g_pre_mix, w_in_b)


def _dattn_kernel(q_ref, k_ref, vt_ref, lq1_ref, lk1_ref, lq2_ref, lk2_ref, gsub_ref,
                  o_ref, dbias_ref, cpos_ref, acc_ref, *bufs):
    tbufs, pbufs = bufs[:DA_HEADS], bufs[DA_HEADS:]
    qi = pl.program_id(1)
    slopes = [LOG2E * 2.0 ** (-8.0 * (h + 1) / DA_HEADS) for h in range(DA_HEADS)]

    @pl.when((pl.program_id(0) == 0) & (qi == 0))
    def _():
        r = lax.broadcasted_iota(jnp.int32, (TK, 128), 0).astype(F32)
        kl = lax.broadcasted_iota(jnp.int32, (TK, 128), 1)
        c = lax.broadcasted_iota(jnp.int32, (TK, TQ), 0)
        a = lax.broadcasted_iota(jnp.int32, (TK, TQ), 1)
        rel = (a - jnp.abs(a - c)).astype(F32)
        allowed = (c // CHUNK) <= (a // CHUNK)
        for h in range(DA_HEADS):
            pbufs[h][...] = jnp.zeros((TK, 2 * TQ), BF16)
            dbias_ref[h] = jnp.where(allowed, slopes[h] * rel, -jnp.inf)
            x = slopes[h] * r
            hi = x.astype(BF16).astype(F32)
            r1 = x - hi
            mid = r1.astype(BF16).astype(F32)
            lo = r1 - mid
            cpos_ref[h] = jnp.where(kl == 0, hi, jnp.where(kl == 1, mid,
                                    jnp.where(kl == 2, lo, 0.0))).astype(BF16)

    ones_rows = jnp.ones((16, TK), BF16)
    bias_rows = jnp.where(lax.broadcasted_iota(jnp.int32, (128, 2 * TQ), 0) < 3,
                          1.0, 0.0).astype(BF16)
    zq = jnp.zeros((DA_QK, TQ), BF16)
    qqt = []
    for h in range(DA_HEADS):
        qt = q_ref[0, 0, 128 * h:128 * h + 128, :]
        qqt.append(jnp.concatenate([jnp.concatenate([qt[0:DA_QK], zq], axis=1),
                                    jnp.concatenate([zq, qt[DA_QK:]], axis=1),
                                    bias_rows], axis=0))

    def kblock(h, kb):
        return k_ref[0, pl.ds(pl.multiple_of(kb * TK, TK), TK), 128 * h:128 * h + 128]

    def diag_scores(h):
        db = dbias_ref[h]
        s = jnp.dot(kblock(h, qi), qqt[h][0:128], preferred_element_type=F32)
        return s + jnp.concatenate([db, db], axis=1)

    def past_scores(h, kb):
        lhs = jnp.concatenate([kblock(h, kb), cpos_ref[h]], axis=1)
        return jnp.dot(lhs, qqt[h], preferred_element_type=F32)

    def score_stage(h, t):
        tbufs[h][...] = t
        return jnp.max(t, axis=0, keepdims=True)

    def softmax_stage(h, tmax, m, shift):
        m_new = jnp.maximum(m, tmax + shift)
        alpha = jnp.exp2(m - m_new)
        pbufs[h][...] = jnp.exp2(tbufs[h][...] - (m_new - shift)).astype(BF16)
        return alpha, m_new

    def value_stage(h, kb, alpha, l, pending=True):
        lhs = jnp.concatenate([vt_ref[0, kb, 128 * h:128 * h + 128, :], ones_rows], axis=0)
        lhs = jnp.where(pending, lhs, jnp.zeros_like(lhs))
        pv = jnp.dot(lhs, pbufs[h][...], preferred_element_type=F32)
        acc_ref[h] = alpha * acc_ref[h] + pv[0:DA_V]
        return alpha * l + pv[DA_V:DA_V + 1]

    heads = range(DA_HEADS)
    for h in heads:
        acc_ref[h] = jnp.zeros((DA_V, 2 * TQ), F32)
    ms = [jnp.full((1, 2 * TQ), -jnp.inf, F32) for _ in heads]
    ls = [jnp.zeros((1, 2 * TQ), F32) for _ in heads]
    alphas = [jnp.ones((1, 2 * TQ), F32) for _ in heads]
    tmaxs = [score_stage(h, diag_scores(h)) for h in heads]

    def step(n, carry):
        ms, ls, alphas, tmaxs = (list(carry[4 * j:4 * j + 4]) for j in range(4))
        kb_value = jnp.where(n == 1, qi, jnp.maximum(n - 2, 0))
        kb_score = jnp.minimum(n, jnp.maximum(qi - 1, 0))
        rel = jnp.where(n == 0, 0, (n - 1 - qi) * TK).astype(F32)
        for h in heads:
            ls[h] = value_stage(h, kb_value, alphas[h], ls[h], pending=n > 0)
        for h in heads:
            alphas[h], ms[h] = softmax_stage(h, tmaxs[h], ms[h], slopes[h] * rel)
            tmaxs[h] = score_stage(h, past_scores(h, kb_score))
        return tuple(ms) + tuple(ls) + tuple(alphas) + tuple(tmaxs)

    carry = lax.fori_loop(0, qi + 1, step, tuple(ms) + tuple(ls) + tuple(alphas) + tuple(tmaxs))
    ms, ls, alphas = (list(carry[4 * j:4 * j + 4]) for j in range(3))
    kb_last = jnp.maximum(qi - 1, 0)
    for h in heads:
        ls[h] = value_stage(h, kb_last, alphas[h], ls[h])
    carry = [x for h in heads for x in (ms[h], ls[h])]

    lam = (jnp.exp(jnp.sum(lq1_ref[...] * lk1_ref[...], axis=-1, keepdims=True))
           - jnp.exp(jnp.sum(lq2_ref[...] * lk2_ref[...], axis=-1, keepdims=True))
           + LAMBDA_INIT)
    for h in range(DA_HEADS):
        inv = 1.0 / carry[2 * h + 1]
        acc = acc_ref[h]
        ot = acc[:, :TQ] * inv[:, :TQ] - lam * (acc[:, TQ:] * inv[:, TQ:])
        ot = ot * lax.rsqrt(jnp.mean(ot * ot, axis=0, keepdims=True) + RMS_EPS)
        o = ot.T * (gsub_ref[...] * (1.0 - LAMBDA_INIT))
        o_ref[0, :, 128 * h:128 * h + 128] = o.astype(BF16)


def _dattn_call(qt, k, vt, lam_q1, lam_k1, lam_q2, lam_k2, g_sub):
    b, s, _ = k.shape
    nkv = s // TK
    vec = lambda n: pl.BlockSpec((1, n), lambda bi, i: (0, 0))
    return pl.pallas_call(
        _dattn_kernel,
        grid=(b, s // TQ),
        in_specs=[pl.BlockSpec((1, 1, DA_WIDTH, TQ), lambda bi, i: (bi, i, 0, 0)),
                  pl.BlockSpec((1, s, DA_WIDTH), lambda bi, i: (bi, 0, 0)),
                  pl.BlockSpec((1, nkv, DA_WIDTH, TK), lambda bi, i: (bi, 0, 0, 0)),
                  vec(DA_QK), vec(DA_QK), vec(DA_QK), vec(DA_QK), vec(DA_V)],
        out_specs=pl.BlockSpec((1, TQ, DA_WIDTH), lambda bi, i: (bi, i, 0)),
        out_shape=jax.ShapeDtypeStruct((b, s, DA_WIDTH), BF16),
        scratch_shapes=[pltpu.VMEM((DA_HEADS, TK, TQ), F32),
                        pltpu.VMEM((DA_HEADS, TK, 128), BF16),
                        pltpu.VMEM((DA_HEADS, DA_V, 2 * TQ), F32)]
                       + [pltpu.VMEM((TK, 2 * TQ), F32) for _ in range(DA_HEADS)]
                       + [pltpu.VMEM((TK, 2 * TQ), BF16) for _ in range(DA_HEADS)],
        compiler_params=pltpu.CompilerParams(
            dimension_semantics=("arbitrary", "arbitrary"),
            vmem_limit_bytes=VMEM_LIMIT_BYTES),
        name="dattn",
    )(qt, k, vt, lam_q1, lam_k1, lam_q2, lam_k2, g_sub)


def _ffn_kernel(oda_ref, oret_ref, x_ref, mod_ref, gpm_ref, gpf_ref, gpo_ref,
                wout_ref, wup_ref, cw_ref, cb_ref, wdn_ref, o_ref, perm_ref, f_ref, carry_ref):
    i = pl.program_id(1)
    n_sub = len(FFN_SUBS)
    starts = [sum(FFN_SUBS[:t]) for t in range(n_sub)]
    assert sum(FFN_SUBS) == x_ref.shape[1]

    def sub_rows(t):
        return slice(starts[t], starts[t] + FFN_SUBS[t])

    @pl.when(i == 0)
    def _():
        carry_ref[...] = jnp.zeros_like(carry_ref)

    def permute_rows(a, slot):
        n = a.shape[1] // LANES
        for s in range(n):
            perm_ref[slot, s, 0:a.shape[0], :] = a[:, LANES * s:LANES * (s + 1)]
        rows = []
        for g in range(a.shape[0] // PERM_GROUP):
            for r in range(SUBLANES):
                rows.append(jnp.concatenate(
                    [perm_ref[slot, s, pl.ds(PERM_GROUP * g + r, SUBLANES, stride=SUBLANES), :]
                     for s in range(n)], axis=1))
        return jnp.concatenate(rows, axis=0)

    gt1 = mod_ref[0, 2:3, :]
    sh2 = mod_ref[0, 3:4, :]
    sc2 = mod_ref[0, 4:5, :]
    gt2 = mod_ref[0, 5:6, :]
    sub0 = lax.broadcasted_iota(jnp.int32, (SUBLANES, FF_CW), 0) == 0
    n_chunks = D_FF // FF_CW

    def pre(t):
        rows = sub_rows(t)
        mix = (jnp.dot(oda_ref[0, rows, :], wout_ref[0:DA_WIDTH, :], preferred_element_type=F32)
               + jnp.dot(oret_ref[0, rows, :], wout_ref[DA_WIDTH:, :],
                         preferred_element_type=F32))
        x1 = x_ref[0, rows, :] + _rms(mix) * (gt1 * gpm_ref[...])
        o_ref[0, rows, :] = x1
        return permute_rows(_rms(x1) * (gpf_ref[...] * (1.0 + sc2)) + sh2, t % 2).astype(BF16)

    def conv(y, c0):
        cols = slice(c0, c0 + FF_CW)
        w0, w1, w2, cb = cw_ref[0:1, cols], cw_ref[1:2, cols], cw_ref[2:3, cols], cb_ref[:, cols]
        prev7 = carry_ref[0, :, cols]
        prev6 = carry_ref[1, :, cols]
        out = []
        g1, g2 = PERM_GROUP - SUBLANES, PERM_GROUP - 2 * SUBLANES
        for g in range(y.shape[0] // PERM_GROUP):
            yg = y[PERM_GROUP * g:PERM_GROUP * (g + 1)]
            rot7 = pltpu.roll(yg[g1:PERM_GROUP], 1, 0)
            rot6 = pltpu.roll(yg[g2:g1], 1, 0)
            sp1 = jnp.where(sub0, prev7, rot7)
            sp2 = jnp.where(sub0, prev6, rot6)
            y1 = jnp.concatenate([sp1, yg[0:g1]], axis=0)
            y2 = jnp.concatenate([sp2, sp1, yg[0:g2]], axis=0)
            out.append(cb + w0 * y2 + w1 * y1 + w2 * yg)
            prev7, prev6 = rot7, rot6
        carry_ref[0, :, cols] = prev7
        carry_ref[1, :, cols] = prev6
        return jnp.concatenate(out, axis=0)

    def mlp(t, h):
        def up(c):
            return [jnp.dot(h, wup_ref[:, c0:c0 + FF_CW], preferred_element_type=F32)
                    for c0 in (c * FF_CW, D_FF + c * FF_CW)]
        ys = up(0)
        for c in range(n_chunks):
            nxt = up(c + 1) if c + 1 < n_chunks else None
            ug = conv(ys[0], c * FF_CW)
            uv = conv(ys[1], D_FF + c * FF_CW)
            f_ref[sub_rows(t), c * FF_CW:(c + 1) * FF_CW] = (
                ug * jax.nn.sigmoid(ug) * uv).astype(BF16)
            ys = nxt
        return jnp.dot(f_ref[sub_rows(t), :], wdn_ref[...], preferred_element_type=F32)

    def post(t, acc):
        rows = sub_rows(t)
        o_ref[0, rows, :] = (o_ref[0, rows, :]
                             + permute_rows(_rms(acc), 2 + t % 2) * (gt2 * gpo_ref[...]))

    hs = [pre(0)]
    for t in range(n_sub):
        if t + 1 < n_sub:
            hs.append(pre(t + 1))
        acc = mlp(t, hs[t])
        post(t, acc)


def _ffn_call(o_da, o_ret, x, mod, g_post_mix, g_pre_ffn, g_post_ffn, w_out_b, w_up_b,
              conv_w, conv_b, w_down_b):
    b, s, d = x.shape
    tm = TM_FFN
    const = dict(pipeline_mode=pl.Buffered(1))
    vec = lambda n: pl.BlockSpec((1, n), lambda bi, i: (0, 0))
    return pl.pallas_call(
        _ffn_kernel,
        grid=(b, s // tm),
        in_specs=[pl.BlockSpec((1, tm, DA_WIDTH), lambda bi, i: (bi, i, 0)),
                  pl.BlockSpec((1, tm, RET_WIDTH), lambda bi, i: (bi, i, 0)),
                  pl.BlockSpec((1, tm, d), lambda bi, i: (bi, i, 0)),
                  pl.BlockSpec((1, N_MOD, d), lambda bi, i: (bi, 0, 0)),
                  vec(d), vec(d), vec(d),
                  pl.BlockSpec((d, d), lambda bi, i: (0, 0), **const),
                  pl.BlockSpec((d, 2 * D_FF), lambda bi, i: (0, 0), **const),
                  pl.BlockSpec((3, 2 * D_FF), lambda bi, i: (0, 0)),
                  pl.BlockSpec((1, 2 * D_FF), lambda bi, i: (0, 0)),
                  pl.BlockSpec((D_FF, d), lambda bi, i: (0, 0), **const)],
        out_specs=pl.BlockSpec((1, tm, d), lambda bi, i: (bi, i, 0)),
        out_shape=jax.ShapeDtypeStruct((b, s, d), F32),
        scratch_shapes=[pltpu.VMEM((4, d // LANES, max(FFN_SUBS), LANES), F32),
                        pltpu.VMEM((tm, D_FF), BF16),
                        pltpu.VMEM((2, SUBLANES, 2 * D_FF), F32)],
        compiler_params=pltpu.CompilerParams(dimension_semantics=("arbitrary", "arbitrary"),
                                             vmem_limit_bytes=VMEM_LIMIT_BYTES),
        name="ffn",
    )(o_da, o_ret, x, mod, g_post_mix, g_pre_ffn, g_post_ffn, w_out_b, w_up_b,
      conv_w, conv_b, w_down_b)


def kernel(x, c, w_ada, b_ada, g_pre_mix, w_in, lam_q1, lam_k1, lam_q2, lam_k2, g_da_subln,
           w_out, g_post_mix, g_pre_ffn, w_up, conv_w, conv_b, w_down, g_post_ffn):
    b, s, d = x.shape
    depth = w_ada.shape[0]
    assert depth == 1 and d == D_MODEL and s % TM_IN == 0 and s % TQ == 0
    for l in range(depth):
        mod = _mod_call(c, w_ada[l], b_ada[l]).reshape(b, N_MOD, d)
        w_in_b = w_in[l].astype(BF16)
        qt, k, vt, o_ret = _inproj_call(x, mod, g_pre_mix[l][None], w_in_b)
        o_da = _dattn_call(qt, k, vt, lam_q1[l][None], lam_k1[l][None], lam_q2[l][None],
                           lam_k2[l][None], g_da_subln[l][None])
        x = _ffn_call(o_da, o_ret, x, mod, g_post_mix[l][None], g_pre_ffn[l][None],
                      g_post_ffn[l][None], w_out[l].astype(BF16), w_up[l].astype(BF16),
                      conv_w[l], conv_b[l][None], w_down[l].astype(BF16))
    return x
```

```python
import math

import jax
import jax.numpy as jnp
from jax import lax
from jax.experimental import pallas as pl
from jax.experimental.pallas import tpu as pltpu

D_MODEL = 1024
CHUNK = 64
DA_HEADS = 4
DA_QK = 64
DA_V = 128
DA_WIDTH = 512
RET_HEADS = 4
RET_QK = 64
RET_V = 128
RET_WIDTH = 512
IN_WIDTH = 3072
COL_DA_Q = 0
COL_DA_K = COL_DA_Q + DA_HEADS * 2 * DA_QK
COL_DA_V = COL_DA_K + DA_HEADS * 2 * DA_QK
COL_RET_Q = COL_DA_V + DA_HEADS * DA_V
COL_RET_K = COL_RET_Q + RET_HEADS * RET_QK
COL_RET_V = COL_RET_K + RET_HEADS * RET_QK
COL_RET_G = COL_RET_V + RET_HEADS * RET_V
assert COL_RET_G + RET_HEADS * RET_V == IN_WIDTH
D_FF = 2816
N_MOD = 6
RMS_EPS = 1e-6
LAMBDA_INIT = 0.8 - 0.6 * math.exp(-0.3 * 0)

LOG2E = 1.4426950408889634
Q_FOLD = (DA_QK ** -0.5) * LOG2E
RET_K_FOLD = RET_QK ** -0.5

VMEM_LIMIT_BYTES = 56 * 1024 * 1024

BF16 = jnp.bfloat16
F32 = jnp.float32

LANES = 128
SUBLANES = 8
PERM_GROUP = SUBLANES * SUBLANES

TM_IN = 1024
TQ = 512
TK = 512
RET_L = 256
TM_FFN = 1024
FFN_SUBS = (192, 320, 256, 256)
FF_CW = 256
assert all(n % PERM_GROUP == 0 for n in FFN_SUBS) and sum(FFN_SUBS) == TM_FFN


def _rms(x):
    return x * lax.rsqrt(jnp.mean(x * x, axis=-1, keepdims=True) + RMS_EPS)


def _mod_kernel(c_ref, w_ref, b_ref, o_ref):
    c = c_ref[...]
    a = c * jax.nn.sigmoid(c)
    o_ref[...] = jnp.dot(a.astype(BF16), w_ref[...].astype(BF16),
                         preferred_element_type=F32) + b_ref[...]


def _mod_call(c, w_ada, b_ada):
    b, d = c.shape
    n = w_ada.shape[1]
    tn = 1024
    return pl.pallas_call(
        _mod_kernel,
        grid=(n // tn,),
        in_specs=[pl.BlockSpec((b, d), lambda j: (0, 0)),
                  pl.BlockSpec((d, tn), lambda j: (0, j)),
                  pl.BlockSpec((1, tn), lambda j: (0, j))],
        out_specs=pl.BlockSpec((b, tn), lambda j: (0, j)),
        out_shape=jax.ShapeDtypeStruct((b, n), F32),
        compiler_params=pltpu.CompilerParams(dimension_semantics=("arbitrary",),
                                             vmem_limit_bytes=VMEM_LIMIT_BYTES),
        name="mod",
    )(c, w_ada, b_ada.reshape(1, n))


def _inproj_kernel(x_ref, mod_ref, g_ref, w_ref, qt_ref, k_ref, vt_ref, oret_ref,
                   dec_ref, rowdec_ref, state_ref, wqt_ref, wvt_ref):
    i = pl.program_id(1)
    tm = x_ref.shape[1]
    L = RET_L
    lgs = [math.log(1.0 - 2.0 ** (-5.0 - h)) for h in range(RET_HEADS)]

    @pl.when(i == 0)
    def _():
        state_ref[...] = jnp.zeros_like(state_ref)

    @pl.when((pl.program_id(0) == 0) & (i == 0))
    def _():
        row = lax.broadcasted_iota(jnp.int32, (L, L), 0)
        col = lax.broadcasted_iota(jnp.int32, (L, L), 1)
        d = row - col
        same = (row // CHUNK) == (col // CHUNK)
        past = (col // CHUNK) < (row // CHUNK)
        e = jnp.where(same, jnp.abs(d), d).astype(F32)
        r = lax.broadcasted_iota(jnp.int32, (L, 128), 0).astype(F32)
        for h in range(RET_HEADS):
            dec_ref[h] = jnp.where(same | past, jnp.exp(lgs[h] * e), 0.0)
            rowdec_ref[0, h] = jnp.exp(lgs[h] * (r + 1.0))
            rowdec_ref[1, h] = jnp.exp(lgs[h] * (L - 1.0 - r))
        wqt_ref[...] = w_ref[:, COL_DA_Q:COL_DA_K].astype(F32).T.astype(BF16)
        wvt_ref[...] = w_ref[:, COL_DA_V:COL_RET_Q].astype(F32).T.astype(BF16)

    x = x_ref[0]
    sh = mod_ref[0, 0:1, :]
    sc = mod_ref[0, 1:2, :]
    hb = (_rms(x) * (g_ref[...] * (1.0 + sc)) + sh).astype(BF16)

    def proj(lo, hi):
        return jnp.dot(hb, w_ref[:, lo:hi], preferred_element_type=F32)

    rq = proj(COL_RET_Q, COL_RET_K).astype(BF16)
    rk = (proj(COL_RET_K, COL_RET_V) * RET_K_FOLD).astype(BF16)
    rv = proj(COL_RET_V, COL_RET_G).astype(BF16)
    rg = proj(COL_RET_G, IN_WIDTH)

    lane = lax.broadcasted_iota(jnp.int32, (L, 128), 1)
    nt = (((1,), (1,)), ((), ()))
    blocks = [(t, h) for t in range(tm // L) for h in range(RET_HEADS)]
    qz, kz, vh, sb, kd = {}, {}, {}, {}, {}
    for t, h in blocks:
        rows = slice(t * L, (t + 1) * L)
        pair = slice(128 * (h // 2), 128 * (h // 2) + 128)
        lo = RET_QK * (h % 2)
        sel = (lane >= lo) & (lane < lo + RET_QK)
        qp, kp = rq[rows, pair], rk[rows, pair]
        qz[t, h] = jnp.where(sel, qp, jnp.zeros_like(qp))
        kz[t, h] = jnp.where(sel, kp, jnp.zeros_like(kp))
        vh[t, h] = rv[rows, 128 * h:128 * h + 128]
        s = lax.dot_general(qz[t, h], kz[t, h], nt, preferred_element_type=F32)
        sb[t, h] = (s * dec_ref[h]).astype(BF16)
        kd[t, h] = (kz[t, h].astype(F32) * rowdec_ref[1, h]).astype(BF16)

    def proj_t(wt_ref, out_ref, fold):
        t = lax.dot_general(wt_ref[...], hb, nt, preferred_element_type=F32)
        t = (t if fold is None else t * fold).astype(BF16)
        for j in range(out_ref.shape[1]):
            out_ref[0, j] = t[:, j * TK:(j + 1) * TK]

    proj_t(wqt_ref, qt_ref, Q_FOLD)

    def retention_block(t):
        rows = slice(t * L, (t + 1) * L)
        for h in range(RET_HEADS):
            st = state_ref[h]
            o = jnp.dot(sb[t, h], vh[t, h], preferred_element_type=F32)
            o = o + rowdec_ref[0, h] * jnp.dot(qz[t, h], st.astype(BF16),
                                               preferred_element_type=F32)
            u = lax.dot_general(kd[t, h], vh[t, h], (((0,), (0,)), ((), ())),
                                preferred_element_type=F32)
            state_ref[h] = math.exp(lgs[h] * L) * st + u
            g = rg[rows, 128 * h:128 * h + 128]
            oret_ref[0, rows, 128 * h:128 * h + 128] = (
                _rms(o) * (g * jax.nn.sigmoid(g))).astype(BF16)

    retention_block(0)
    k_ref[0] = proj(COL_DA_K, COL_DA_V).astype(BF16)
    for t in range(1, tm // L):
        retention_block(t)
    proj_t(wvt_ref, vt_ref, None)


def _inproj_call(x, mod, g_pre_mix, w_in_b):
    b, s, d = x.shape
    tm = TM_IN
    const = dict(pipeline_mode=pl.Buffered(1))
    return pl.pallas_call(
        _inproj_kernel,
        grid=(b, s // tm),
        in_specs=[pl.BlockSpec((1, tm, d), lambda bi, i: (bi, i, 0)),
                  pl.BlockSpec((1, N_MOD, d), lambda bi, i: (bi, 0, 0)),
                  pl.BlockSpec((1, d), lambda bi, i: (0, 0)),
                  pl.BlockSpec((d, IN_WIDTH), lambda bi, i: (0, 0), **const)],
        out_specs=[pl.BlockSpec((1, tm // TK, DA_WIDTH, TK), lambda bi, i: (bi, i, 0, 0)),
                   pl.BlockSpec((1, tm, DA_WIDTH), lambda bi, i: (bi, i, 0)),
                   pl.BlockSpec((1, tm // TK, DA_WIDTH, TK), lambda bi, i: (bi, i, 0, 0)),
                   pl.BlockSpec((1, tm, RET_WIDTH), lambda bi, i: (bi, i, 0))],
        out_shape=[jax.ShapeDtypeStruct((b, s // TK, DA_WIDTH, TK), BF16),
                   jax.ShapeDtypeStruct((b, s, DA_WIDTH), BF16),
                   jax.ShapeDtypeStruct((b, s // TK, DA_WIDTH, TK), BF16),
                   jax.ShapeDtypeStruct((b, s, RET_WIDTH), BF16)],
        scratch_shapes=[pltpu.VMEM((RET_HEADS, RET_L, RET_L), F32),
                        pltpu.VMEM((2, RET_HEADS, RET_L, 128), F32),
                        pltpu.VMEM((RET_HEADS, 128, RET_V), F32),
                        pltpu.VMEM((DA_WIDTH, d), BF16),
                        pltpu.VMEM((DA_WIDTH, d), BF16)],
        compiler_params=pltpu.CompilerParams(dimension_semantics=("arbitrary", "arbitrary"),
                                             vmem_limit_bytes=VMEM_LIMIT_BYTES),
        name="inproj",
    )(x, mod, g_pre_mix, w_in_b)


def _dattn_kernel(q_ref, k_ref, vt_ref, lq1_ref, lk1_ref, lq2_ref, lk2_ref, gsub_ref,
                  o_ref, dbias_ref, cpos_ref, acc_ref, *bufs):
    tbufs, pbufs = bufs[:DA_HEADS], bufs[DA_HEADS:]
    qi = pl.program_id(1)
    slopes = [LOG2E * 2.0 ** (-8.0 * (h + 1) / DA_HEADS) for h in range(DA_HEADS)]

    @pl.when((pl.program_id(0) == 0) & (qi == 0))
    def _():
        r = lax.broadcasted_iota(jnp.int32, (TK, 128), 0).astype(F32)
        kl = lax.broadcasted_iota(jnp.int32, (TK, 128), 1)
        c = lax.broadcasted_iota(jnp.int32, (TK, TQ), 0)
        a = lax.broadcasted_iota(jnp.int32, (TK, TQ), 1)
        rel = (a - jnp.abs(a - c)).astype(F32)
        allowed = (c // CHUNK) <= (a // CHUNK)
        for h in range(DA_HEADS):
            dbias_ref[h] = jnp.where(allowed, slopes[h] * rel, -jnp.inf)
            x = slopes[h] * r
            hi = x.astype(BF16).astype(F32)
            r1 = x - hi
            mid = r1.astype(BF16).astype(F32)
            lo = r1 - mid
            cpos_ref[h] = jnp.where(kl == 0, hi, jnp.where(kl == 1, mid,
                                    jnp.where(kl == 2, lo, 0.0))).astype(BF16)

    ones_rows = jnp.ones((16, TK), BF16)
    bias_rows = jnp.where(lax.broadcasted_iota(jnp.int32, (128, 2 * TQ), 0) < 3,
                          1.0, 0.0).astype(BF16)
    zq = jnp.zeros((DA_QK, TQ), BF16)
    qqt = []
    for h in range(DA_HEADS):
        qt = q_ref[0, 0, 128 * h:128 * h + 128, :]
        qqt.append(jnp.concatenate([jnp.concatenate([qt[0:DA_QK], zq], axis=1),
                                    jnp.concatenate([zq, qt[DA_QK:]], axis=1),
                                    bias_rows], axis=0))

    def kblock(h, kb):
        return k_ref[0, pl.ds(pl.multiple_of(kb * TK, TK), TK), 128 * h:128 * h + 128]

    def diag_scores(h):
        db = dbias_ref[h]
        s = jnp.dot(kblock(h, qi), qqt[h][0:128], preferred_element_type=F32)
        return s + jnp.concatenate([db, db], axis=1)

    def past_scores(h, kb):
        lhs = jnp.concatenate([kblock(h, kb), cpos_ref[h]], axis=1)
        return jnp.dot(lhs, qqt[h], preferred_element_type=F32)

    def score_stage(h, t):
        tbufs[h][...] = t
        return jnp.max(t, axis=0, keepdims=True)

    def softmax_stage(h, tmax, m, shift):
        m_new = jnp.maximum(m, tmax + shift)
        alpha = jnp.exp2(m - m_new)
        pbufs[h][...] = jnp.exp2(tbufs[h][...] - (m_new - shift)).astype(BF16)
        return alpha, m_new

    def value_stage(h, kb, alpha, l):
        lhs = jnp.concatenate([vt_ref[0, kb, 128 * h:128 * h + 128, :], ones_rows], axis=0)
        pv = jnp.dot(lhs, pbufs[h][...], preferred_element_type=F32)
        acc_ref[h] = alpha * acc_ref[h] + pv[0:DA_V]
        return alpha * l + pv[DA_V:DA_V + 1]

    heads = range(DA_HEADS)
    for h in heads:
        acc_ref[h] = jnp.zeros((DA_V, 2 * TQ), F32)
    ms = [jnp.full((1, 2 * TQ), -jnp.inf, F32) for _ in heads]
    ls = [jnp.zeros((1, 2 * TQ), F32) for _ in heads]
    tmaxs = [score_stage(h, diag_scores(h)) for h in heads]

    def step(n, carry):
        ms, ls, tmaxs = (list(carry[DA_HEADS * j:DA_HEADS * (j + 1)]) for j in range(3))
        kb_cur = jnp.where(n == 0, qi, jnp.maximum(n - 1, 0))
        kb_next = jnp.minimum(n, jnp.maximum(qi - 1, 0))
        rel = jnp.where(n == 0, 0, (n - 1 - qi) * TK).astype(F32)
        alphas = []
        for h in heads:
            alpha, ms[h] = softmax_stage(h, tmaxs[h], ms[h], slopes[h] * rel)
            alphas.append(alpha)
            tmaxs[h] = score_stage(h, past_scores(h, kb_next))
        for h in heads:
            ls[h] = value_stage(h, kb_cur, alphas[h], ls[h])
        return tuple(ms) + tuple(ls) + tuple(tmaxs)

    carry = lax.fori_loop(0, qi + 1, step, tuple(ms) + tuple(ls) + tuple(tmaxs))
    ms, ls = (list(carry[DA_HEADS * j:DA_HEADS * (j + 1)]) for j in range(2))
    carry = [x for h in heads for x in (ms[h], ls[h])]

    lam = (jnp.exp(jnp.sum(lq1_ref[...] * lk1_ref[...], axis=-1, keepdims=True))
           - jnp.exp(jnp.sum(lq2_ref[...] * lk2_ref[...], axis=-1, keepdims=True))
           + LAMBDA_INIT)
    for h in range(DA_HEADS):
        inv = 1.0 / carry[2 * h + 1]
        acc = acc_ref[h]
        ot = acc[:, :TQ] * inv[:, :TQ] - lam * (acc[:, TQ:] * inv[:, TQ:])
        ot = ot * lax.rsqrt(jnp.mean(ot * ot, axis=0, keepdims=True) + RMS_EPS)
        o = ot.T * (gsub_ref[...] * (1.0 - LAMBDA_INIT))
        o_ref[0, :, 128 * h:128 * h + 128] = o.astype(BF16)


def _dattn_call(qt, k, vt, lam_q1, lam_k1, lam_q2, lam_k2, g_sub):
    b, s, _ = k.shape
    nkv = s // TK
    vec = lambda n: pl.BlockSpec((1, n), lambda bi, i: (0, 0))
    return pl.pallas_call(
        _dattn_kernel,
        grid=(b, s // TQ),
        in_specs=[pl.BlockSpec((1, 1, DA_WIDTH, TQ), lambda bi, i: (bi, i, 0, 0)),
                  pl.BlockSpec((1, s, DA_WIDTH), lambda bi, i: (bi, 0, 0)),
                  pl.BlockSpec((1, nkv, DA_WIDTH, TK), lambda bi, i: (bi, 0, 0, 0)),
                  vec(DA_QK), vec(DA_QK), vec(DA_QK), vec(DA_QK), vec(DA_V)],
        out_specs=pl.BlockSpec((1, TQ, DA_WIDTH), lambda bi, i: (bi, i, 0)),
        out_shape=jax.ShapeDtypeStruct((b, s, DA_WIDTH), BF16),
        scratch_shapes=[pltpu.VMEM((DA_HEADS, TK, TQ), F32),
                        pltpu.VMEM((DA_HEADS, TK, 128), BF16),
                        pltpu.VMEM((DA_HEADS, DA_V, 2 * TQ), F32)]
                       + [pltpu.VMEM((TK, 2 * TQ), F32) for _ in range(DA_HEADS)]
                       + [pltpu.VMEM((TK, 2 * TQ), BF16) for _ in range(DA_HEADS)],
        compiler_params=pltpu.CompilerParams(
            dimension_semantics=("arbitrary", "arbitrary"),
            vmem_limit_bytes=VMEM_LIMIT_BYTES),
        name="dattn",
    )(qt, k, vt, lam_q1, lam_k1, lam_q2, lam_k2, g_sub)


def _ffn_kernel(oda_ref, oret_ref, x_ref, mod_ref, gpm_ref, gpf_ref, gpo_ref,
                wout_ref, wup_ref, cw_ref, cb_ref, wdn_ref, o_ref, perm_ref, f_ref, carry_ref):
    i = pl.program_id(1)
    n_sub = len(FFN_SUBS)
    starts = [sum(FFN_SUBS[:t]) for t in range(n_sub)]
    assert sum(FFN_SUBS) == x_ref.shape[1]

    def sub_rows(t):
        return slice(starts[t], starts[t] + FFN_SUBS[t])

    @pl.when(i == 0)
    def _():
        carry_ref[...] = jnp.zeros_like(carry_ref)

    def permute_rows(a, slot):
        n = a.shape[1] // LANES
        for s in range(n):
            perm_ref[slot, s, 0:a.shape[0], :] = a[:, LANES * s:LANES * (s + 1)]
        rows = []
        for g in range(a.shape[0] // PERM_GROUP):
            for r in range(SUBLANES):
                rows.append(jnp.concatenate(
                    [perm_ref[slot, s, pl.ds(PERM_GROUP * g + r, SUBLANES, stride=SUBLANES), :]
                     for s in range(n)], axis=1))
        return jnp.concatenate(rows, axis=0)

    gt1 = mod_ref[0, 2:3, :]
    sh2 = mod_ref[0, 3:4, :]
    sc2 = mod_ref[0, 4:5, :]
    gt2 = mod_ref[0, 5:6, :]
    sub0 = lax.broadcasted_iota(jnp.int32, (SUBLANES, FF_CW), 0) == 0
    n_chunks = D_FF // FF_CW

    def pre(t):
        rows = sub_rows(t)
        mix = (jnp.dot(oda_ref[0, rows, :], wout_ref[0:DA_WIDTH, :], preferred_element_type=F32)
               + jnp.dot(oret_ref[0, rows, :], wout_ref[DA_WIDTH:, :],
                         preferred_element_type=F32))
        x1 = x_ref[0, rows, :] + _rms(mix) * (gt1 * gpm_ref[...])
        o_ref[0, rows, :] = x1
        return permute_rows(_rms(x1) * (gpf_ref[...] * (1.0 + sc2)) + sh2, t % 2).astype(BF16)

    def conv(y, c0):
        cols = slice(c0, c0 + FF_CW)
        w0, w1, w2, cb = cw_ref[0:1, cols], cw_ref[1:2, cols], cw_ref[2:3, cols], cb_ref[:, cols]
        prev7 = carry_ref[0, :, cols]
        prev6 = carry_ref[1, :, cols]
        out = []
        g1, g2 = PERM_GROUP - SUBLANES, PERM_GROUP - 2 * SUBLANES
        for g in range(y.shape[0] // PERM_GROUP):
            yg = y[PERM_GROUP * g:PERM_GROUP * (g + 1)]
            rot7 = pltpu.roll(yg[g1:PERM_GROUP], 1, 0)
            rot6 = pltpu.roll(yg[g2:g1], 1, 0)
            sp1 = jnp.where(sub0, prev7, rot7)
            sp2 = jnp.where(sub0, prev6, rot6)
            y1 = jnp.concatenate([sp1, yg[0:g1]], axis=0)
            y2 = jnp.concatenate([sp2, sp1, yg[0:g2]], axis=0)
            out.append(cb + w0 * y2 + w1 * y1 + w2 * yg)
            prev7, prev6 = rot7, rot6
        carry_ref[0, :, cols] = prev7
        carry_ref[1, :, cols] = prev6
        return jnp.concatenate(out, axis=0)

    def mlp(t, h):
        def up(c):
            return [jnp.dot(h, wup_ref[:, c0:c0 + FF_CW], preferred_element_type=F32)
                    for c0 in (c * FF_CW, D_FF + c * FF_CW)]
        ys = up(0)
        for c in range(n_chunks):
            nxt = up(c + 1) if c + 1 < n_chunks else None
            ug = conv(ys[0], c * FF_CW)
            uv = conv(ys[1], D_FF + c * FF_CW)
            f_ref[sub_rows(t), c * FF_CW:(c + 1) * FF_CW] = (
                ug * jax.nn.sigmoid(ug) * uv).astype(BF16)
            ys = nxt
        return jnp.dot(f_ref[sub_rows(t), :], wdn_ref[...], preferred_element_type=F32)

    def post(t, acc):
        rows = sub_rows(t)
        o_ref[0, rows, :] = (o_ref[0, rows, :]
                             + permute_rows(_rms(acc), 2 + t % 2) * (gt2 * gpo_ref[...]))

    hs = [pre(0)]
    for t in range(n_sub):
        if t + 1 < n_sub:
            hs.append(pre(t + 1))
        acc = mlp(t, hs[t])
        post(t, acc)


def _ffn_call(o_da, o_ret, x, mod, g_post_mix, g_pre_ffn, g_post_ffn, w_out_b, w_up_b,
              conv_w, conv_b, w_down_b):
    b, s, d = x.shape
    tm = TM_FFN
    const = dict(pipeline_mode=pl.Buffered(1))
    vec = lambda n: pl.BlockSpec((1, n), lambda bi, i: (0, 0))
    return pl.pallas_call(
        _ffn_kernel,
        grid=(b, s // tm),
        in_specs=[pl.BlockSpec((1, tm, DA_WIDTH), lambda bi, i: (bi, i, 0)),
                  pl.BlockSpec((1, tm, RET_WIDTH), lambda bi, i: (bi, i, 0)),
                  pl.BlockSpec((1, tm, d), lambda bi, i: (bi, i, 0)),
                  pl.BlockSpec((1, N_MOD, d), lambda bi, i: (bi, 0, 0)),
                  vec(d), vec(d), vec(d),
                  pl.BlockSpec((d, d), lambda bi, i: (0, 0), **const),
                  pl.BlockSpec((d, 2 * D_FF), lambda bi, i: (0, 0), **const),
                  pl.BlockSpec((3, 2 * D_FF), lambda bi, i: (0, 0)),
                  pl.BlockSpec((1, 2 * D_FF), lambda bi, i: (0, 0)),
                  pl.BlockSpec((D_FF, d), lambda bi, i: (0, 0), **const)],
        out_specs=pl.BlockSpec((1, tm, d), lambda bi, i: (bi, i, 0)),
        out_shape=jax.ShapeDtypeStruct((b, s, d), F32),
        scratch_shapes=[pltpu.VMEM((4, d // LANES, max(FFN_SUBS), LANES), F32),
                        pltpu.VMEM((tm, D_FF), BF16),
                        pltpu.VMEM((2, SUBLANES, 2 * D_FF), F32)],
        compiler_params=pltpu.CompilerParams(dimension_semantics=("arbitrary", "arbitrary"),
                                             vmem_limit_bytes=VMEM_LIMIT_BYTES),
        name="ffn",
    )(o_da, o_ret, x, mod, g_post_mix, g_pre_ffn, g_post_ffn, w_out_b, w_up_b,
      conv_w, conv_b, w_down_b)


def kernel(x, c, w_ada, b_ada, g_pre_mix, w_in, lam_q1, lam_k1, lam_q2, lam_k2, g_da_subln,
           w_out, g_post_mix, g_pre_ffn, w_up, conv_w, conv_b, w_down, g_post_ffn):
    b, s, d = x.shape
    depth = w_ada.shape[0]
    assert depth == 1 and d == D_MODEL and s % TM_IN == 0 and s % TQ == 0
    for l in range(depth):
        mod = _mod_call(c, w_ada[l], b_ada[l]).reshape(b, N_MOD, d)
        w_in_b = w_in[l].astype(BF16)
        qt, k, vt, o_ret = _inproj_call(x, mod, g_pre_mix[l][None], w_in_b)
        o_da = _dattn_call(qt, k, vt, lam_q1[l][None], lam_k1[l][None], lam_q2[l][None],
                           lam_k2[l][None], g_da_subln[l][None])
        x = _ffn_call(o_da, o_ret, x, mod, g_post_mix[l][None], g_pre_ffn[l][None],
                      g_post_ffn[l][None], w_out[l].astype(BF16), w_up[l].astype(BF16),
                      conv_w[l], conv_b[l][None], w_down[l].astype(BF16))
    return x
```
